```python
import jax, jax.numpy as jnp
from jax import lax
import numpy as np

D_MODEL = 1024
BATCH = 2
SEQ = 8192
DEPTH = 2
DEC_BATCH = 32
DEC_SEQ = 64
PAST_LEN = 4096

CHUNK = 64
N_META = 16
D_CONV = D_MODEL
CONV_K = 31
D_ML = 2 * D_MODEL
ML_HEADS = 4
ML_HEAD_DIM = D_ML // ML_HEADS
QK_CONV_K = 4
EPS = 1e-6
SPLITS = tuple(int(s) for s in np.cumsum([D_CONV, D_CONV, D_CONV, D_ML, D_ML, D_ML, D_ML, D_ML, ML_HEADS, ML_HEADS, D_MODEL]))
D_IN = 3 * D_CONV + 5 * D_ML + 2 * ML_HEADS + 2 * D_MODEL

kernel_name = 'hybrid_conformer_mlstm_stream_step'


def rms_norm(x, g):
    xf = x.astype(jnp.float32)
    y = xf * lax.rsqrt(jnp.mean(xf * xf, axis=-1, keepdims=True) + EPS)
    return (y * g.astype(jnp.float32)).astype(x.dtype)


def layer_norm(x, g, b):
    xf = x.astype(jnp.float32)
    mu = jnp.mean(xf, axis=-1, keepdims=True)
    xc = xf - mu
    y = xc * lax.rsqrt(jnp.mean(xc * xc, axis=-1, keepdims=True) + EPS)
    return (y * g.astype(jnp.float32) + b.astype(jnp.float32)).astype(x.dtype)


def causal_dwconv(buf, x, w, b):
    xp = jnp.concatenate([buf.astype(x.dtype), x], axis=1)
    y = lax.conv_general_dilated(xp, w[:, None, :].astype(xp.dtype), window_strides=(1,), padding='VALID',
                                 dimension_numbers=('NWC', 'WIO', 'NWC'), feature_group_count=x.shape[-1])
    return y + b, xp[:, -(w.shape[0] - 1):]


def mlstm_chunk(state, q, k, v, ipre, logf):
    C, n, m = state
    L = q.shape[1]
    causal = jnp.tril(jnp.ones((L, L), dtype=bool))[None, :, :, None]
    b = jnp.cumsum(logf, axis=1)
    inter = b + m[:, None, :]
    dmat = b[:, :, None, :] - b[:, None, :, :] + ipre[:, None, :, :]
    dmat = jnp.where(causal, dmat, -jnp.inf)
    m_t = jnp.maximum(inter, jnp.max(dmat, axis=2))
    w_intra = jnp.exp(dmat - m_t[:, :, None, :])
    w_inter = jnp.exp(inter - m_t)
    s = jnp.einsum('bthd,bshd->btsh', q, k) * w_intra
    num = jnp.einsum('btsh,bshd->bthd', s, v) + w_inter[..., None] * jnp.einsum('bhed,bthd->bthe', C, q)
    den = jnp.sum(s, axis=2) + w_inter * jnp.einsum('bhd,bthd->bth', n, q)
    h = num / jnp.maximum(jnp.abs(den), jnp.exp(-m_t))[..., None]
    m_new = m_t[:, -1]
    decay = jnp.exp(b[:, -1] + m - m_new)
    w_s = jnp.exp(b[:, -1:] - b + ipre - m_new[:, None])
    C_new = decay[..., None, None] * C + jnp.einsum('bsh,bshe,bshd->bhed', w_s, v, k)
    n_new = decay[..., None] * n + jnp.einsum('bsh,bshd->bhd', w_s, k)
    return h, (C_new, n_new, m_new)


def mlstm_sequence(state, q, k, v, ipre, logf, lead):
    state = tuple(s.astype(jnp.float32) for s in state)
    outs = []
    if lead > 0:
        h0, state = mlstm_chunk(state, q[:, :lead], k[:, :lead], v[:, :lead], ipre[:, :lead], logf[:, :lead])
        outs.append(h0)
        q, k, v, ipre, logf = q[:, lead:], k[:, lead:], v[:, lead:], ipre[:, lead:], logf[:, lead:]
    bsz, T = q.shape[0], q.shape[1]
    if T <= CHUNK:
        h1, state = mlstm_chunk(state, q, k, v, ipre, logf)
    else:
        nc = T // CHUNK

        def to_blocks(a):
            return jnp.moveaxis(a.reshape((bsz, nc, CHUNK) + a.shape[2:]), 1, 0)

        def step(st, xs):
            h, st = mlstm_chunk(st, *xs)
            return st, h

        state, hs = lax.scan(step, state, (to_blocks(q), to_blocks(k), to_blocks(v), to_blocks(ipre), to_blocks(logf)))
        h1 = jnp.moveaxis(hs, 0, 1).reshape((bsz, T) + q.shape[2:])
    outs.append(h1)
    return jnp.concatenate(outs, axis=1), state


def mixer_layer(x, conv_buf, qk_buf, ml_state, lead, norm_pre, norm_post, w_in, b_in, w_dw, b_dw, ln_g, ln_b,
                w_qkc, b_qkc, f_bias, ml_norm, w_conv_out, w_ml_out, w_out):
    bsz, T, _ = x.shape
    h = rms_norm(x, norm_pre)
    proj = h @ w_in + b_in
    a, gl, z_c, q_pre, k_pre, v, o_pre, z_m, i_pre, f_pre, g_c, g_m = jnp.split(proj, SPLITS, axis=-1)
    u = a * jax.nn.sigmoid(gl)
    u, conv_buf = causal_dwconv(conv_buf, u, w_dw, b_dw)
    u = jax.nn.silu(layer_norm(u, ln_g, ln_b)) * jax.nn.silu(z_c)
    br_c = u @ w_conv_out
    qk, qk_buf = causal_dwconv(qk_buf, jnp.concatenate([q_pre, k_pre], axis=-1), w_qkc, b_qkc)
    q, k = jnp.split(jax.nn.silu(qk), 2, axis=-1)

    def heads(t):
        return t.reshape(bsz, T, ML_HEADS, ML_HEAD_DIM).astype(jnp.float32)

    logf = jax.nn.log_sigmoid((f_pre + f_bias).astype(jnp.float32))
    hc, ml_state = mlstm_sequence(ml_state, heads(q), heads(k) * (ML_HEAD_DIM ** -0.5), heads(v),
                                  i_pre.astype(jnp.float32), logf, lead)
    hc = hc * lax.rsqrt(jnp.mean(hc * hc, axis=-1, keepdims=True) + EPS)
    hc = (hc.reshape(bsz, T, D_ML) * ml_norm.astype(jnp.float32)).astype(x.dtype)
    hm = hc * jax.nn.sigmoid(o_pre) * jax.nn.silu(z_m)
    br_m = hm @ w_ml_out
    y = jax.nn.sigmoid(g_c) * br_c + jax.nn.sigmoid(g_m) * br_m
    out = y @ w_out
    return x + rms_norm(out, norm_post), conv_buf, qk_buf, ml_state


def setup_inputs(seed: int = 0) -> dict:
    key = jax.random.key(seed)
    ks = jax.random.split(key, 32)
    f32 = jnp.float32
    nrm = lambda k, s, sc: jax.random.normal(k, s, f32) * sc
    return {
        'x_prompt': nrm(ks[0], (BATCH, SEQ, D_MODEL), 1.0),
        'x_sample': nrm(ks[1], (DEC_BATCH, DEC_SEQ, D_MODEL), 1.0),
        'state_conv': nrm(ks[2], (DEPTH, DEC_BATCH, CONV_K - 1, D_CONV), 0.5),
        'state_qk_conv': nrm(ks[3], (DEPTH, DEC_BATCH, QK_CONV_K - 1, 2 * D_ML), 1.0),
        'state_C': nrm(ks[4], (DEPTH, DEC_BATCH, ML_HEADS, ML_HEAD_DIM, ML_HEAD_DIM), 0.05),
        'state_n': nrm(ks[5], (DEPTH, DEC_BATCH, ML_HEADS, ML_HEAD_DIM), 0.1),
        'state_m': jax.random.uniform(ks[6], (DEPTH, DEC_BATCH, ML_HEADS), f32, 0.0, 4.0),
        'meta_tokens': nrm(ks[7], (N_META, D_MODEL), 1.0),
        'norm_pre': 1.0 + nrm(ks[8], (DEPTH, D_MODEL), 0.05),
        'norm_post': 1.0 + nrm(ks[9], (DEPTH, D_MODEL), 0.05),
        'w_in': nrm(ks[10], (DEPTH, D_MODEL, D_IN), D_MODEL ** -0.5),
        'b_in': nrm(ks[11], (DEPTH, D_IN), 0.02),
        'w_dw': nrm(ks[12], (DEPTH, CONV_K, D_CONV), CONV_K ** -0.5),
        'b_dw': nrm(ks[13], (DEPTH, D_CONV), 0.02),
        'ln_g': 1.0 + nrm(ks[14], (DEPTH, D_CONV), 0.05),
        'ln_b': nrm(ks[15], (DEPTH, D_CONV), 0.02),
        'w_qk_conv': nrm(ks[16], (DEPTH, QK_CONV_K, 2 * D_ML), QK_CONV_K ** -0.5),
        'b_qk_conv': nrm(ks[17], (DEPTH, 2 * D_ML), 0.02),
        'f_bias': 3.0 + 3.0 * jax.random.uniform(ks[18], (DEPTH, ML_HEADS), f32),
        'ml_norm': 1.0 + nrm(ks[19], (DEPTH, D_ML), 0.05),
        'w_conv_out': nrm(ks[20], (DEPTH, D_CONV, D_MODEL), D_CONV ** -0.5),
        'w_ml_out': nrm(ks[21], (DEPTH, D_ML, D_MODEL), D_ML ** -0.5),
        'w_out': nrm(ks[22], (DEPTH, D_MODEL, D_MODEL), D_MODEL ** -0.5),
    }


def reference(x_prompt, x_sample, state_conv, state_qk_conv, state_C, state_n, state_m, meta_tokens,
              norm_pre, norm_post, w_in, b_in, w_dw, b_dw, ln_g, ln_b, w_qk_conv, b_qk_conv, f_bias,
              ml_norm, w_conv_out, w_ml_out, w_out):
    bp = x_prompt.shape[0]
    meta = jnp.broadcast_to(meta_tokens.astype(x_prompt.dtype)[None], (bp, N_META, D_MODEL))
    xp = jnp.concatenate([meta, x_prompt], axis=1)
    xs = x_sample
    p_conv, p_qk, p_C, p_n, p_m = [], [], [], [], []
    s_conv, s_qk, s_C, s_n, s_m = [], [], [], [], []
    for l in range(DEPTH):
        params = (norm_pre[l], norm_post[l], w_in[l], b_in[l], w_dw[l], b_dw[l], ln_g[l], ln_b[l],
                  w_qk_conv[l], b_qk_conv[l], f_bias[l], ml_norm[l], w_conv_out[l], w_ml_out[l], w_out[l])
        cb0 = jnp.zeros((bp, CONV_K - 1, D_CONV), xp.dtype)
        qb0 = jnp.zeros((bp, QK_CONV_K - 1, 2 * D_ML), xp.dtype)
        ml0 = (jnp.zeros((bp, ML_HEADS, ML_HEAD_DIM, ML_HEAD_DIM), jnp.float32),
               jnp.zeros((bp, ML_HEADS, ML_HEAD_DIM), jnp.float32),
               jnp.zeros((bp, ML_HEADS), jnp.float32))
        xp, cb, qb, (C, n, m) = mixer_layer(xp, cb0, qb0, ml0, N_META, *params)
        p_conv.append(cb); p_qk.append(qb); p_C.append(C); p_n.append(n); p_m.append(m)
        xs, cb, qb, (C, n, m) = mixer_layer(xs, state_conv[l], state_qk_conv[l],
                                            (state_C[l], state_n[l], state_m[l]), 0, *params)
        s_conv.append(cb); s_qk.append(qb); s_C.append(C); s_n.append(n); s_m.append(m)
    return (xp[:, N_META:], xs,
            jnp.stack(p_conv), jnp.stack(p_qk), jnp.stack(p_C), jnp.stack(p_n), jnp.stack(p_m),
            jnp.stack(s_conv), jnp.stack(s_qk), jnp.stack(s_C), jnp.stack(s_n), jnp.stack(s_m))
```

```python
import functools

import jax
import jax.numpy as jnp
from jax import lax
from jax.experimental import pallas as pl
from jax.experimental.pallas import tpu as pltpu

F32 = jnp.float32
BF16 = jnp.bfloat16

D_MODEL = 1024
D_CONV = D_MODEL
CONV_K = 31
D_ML = 2 * D_MODEL
ML_HEADS = 4
ML_HEAD_DIM = D_ML // ML_HEADS
QK_CONV_K = 4
N_META = 16
EPS = 1e-6

LANES = 128
SUBLANES = 8
D_MAIN = 3 * D_CONV + 5 * D_ML + 2 * D_MODEL
GATE_W = LANES
CONV_HIST = 32
QK_HIST = SUBLANES
VMEM_LIMIT = 60 * 1024 * 1024

OFF_A, OFF_GL, OFF_ZC = 0, D_CONV, 2 * D_CONV
OFF_Q = 3 * D_CONV
OFF_K = OFF_Q + D_ML
OFF_V = OFF_K + D_ML
OFF_O = OFF_V + D_ML
OFF_ZM = OFF_O + D_ML
OFF_GC = OFF_ZM + D_ML
OFF_GM = OFF_GC + D_MODEL


def _sigmoid(x):
    return jax.nn.sigmoid(x)


def _silu(x):
    return x * jax.nn.sigmoid(x)


def _inproj_kernel(x_ref, g_ref, w_ref, b_ref, wg_ref, bg_ref, o_ref, og_ref, h_ref):
    @pl.when(pl.program_id(1) == 0)
    def _():
        x = x_ref[...]
        h = x * lax.rsqrt(jnp.mean(x * x, axis=-1, keepdims=True) + EPS) * g_ref[...]
        hb = h.astype(BF16)
        h_ref[...] = hb
        og_ref[...] = jnp.dot(hb, wg_ref[...], preferred_element_type=F32) + bg_ref[...]

    o_ref[...] = jnp.dot(h_ref[...], w_ref[...], preferred_element_type=F32) + b_ref[...]


def _inproj(x2d, g, w_main, b_main, w_gate, b_gate, tm, tn):
    n_tok = x2d.shape[0]
    grid = (n_tok // tm, D_MAIN // tn)
    return pl.pallas_call(
        _inproj_kernel,
        grid=grid,
        in_specs=[
            pl.BlockSpec((tm, D_MODEL), lambda i, j: (i, 0)),
            pl.BlockSpec((1, D_MODEL), lambda i, j: (0, 0)),
            pl.BlockSpec((D_MODEL, tn), lambda i, j: (0, j)),
            pl.BlockSpec((1, tn), lambda i, j: (0, j)),
            pl.BlockSpec((D_MODEL, GATE_W), lambda i, j: (0, 0)),
            pl.BlockSpec((1, GATE_W), lambda i, j: (0, 0)),
        ],
        out_specs=[
            pl.BlockSpec((tm, tn), lambda i, j: (i, j)),
            pl.BlockSpec((tm, GATE_W), lambda i, j: (i, 0)),
        ],
        out_shape=[
            jax.ShapeDtypeStruct((n_tok, D_MAIN), F32),
            jax.ShapeDtypeStruct((n_tok, GATE_W), F32),
        ],
        scratch_shapes=[pltpu.VMEM((tm, D_MODEL), BF16)],
        compiler_params=pltpu.CompilerParams(
            dimension_semantics=("arbitrary", "arbitrary"), vmem_limit_bytes=VMEM_LIMIT),
        name="inproj",
    )(x2d, g, w_main, b_main, w_gate, b_gate)


def _cumsum_rows(x, n_rows):
    row = lax.broadcasted_iota(jnp.int32, x.shape, 0)
    k = 1
    while k < n_rows:
        x = x + jnp.where(row >= k, pltpu.roll(x, k, axis=0), 0.0)
        k *= 2
    return x


def _mixer_kernel(*refs, L, pad, fresh):
    if fresh:
        (x_ref, p_ref, pg_ref,
         wdw, bdw, lng, lnb, wqk, bqk, fb, mln, wco, wmo, wo, npost,
         y_ref, cb_out, qb_out, c_out, n_out, m_out,
         ubuf, cvo, qkbuf, qks, hm_scr) = refs
        cb_in = qb_in = c_in = n_in = m_in = None
    else:
        (x_ref, p_ref, pg_ref, cb_in, qb_in, c_in, n_in, m_in,
         wdw, bdw, lng, lnb, wqk, bqk, fb, mln, wco, wmo, wo, npost,
         y_ref, cb_out, qb_out, c_out, n_out, m_out,
         ubuf, cvo, qkbuf, qks, hm_scr) = refs

    c = pl.program_id(1)

    @pl.when(c == 0)
    def _init():
        ubuf[0:CONV_HIST, :] = jnp.zeros((CONV_HIST, D_CONV), F32)
        qkbuf[0:QK_HIST, :] = jnp.zeros((QK_HIST, 2 * D_ML), F32)
        if fresh:
            c_out[...] = jnp.zeros(c_out.shape, F32)
            n_out[...] = jnp.zeros(n_out.shape, F32)
            m_out[...] = jnp.zeros(m_out.shape, F32)
        else:
            ubuf[CONV_HIST - (CONV_K - 1):CONV_HIST, :] = cb_in[0]
            qkbuf[QK_HIST - (QK_CONV_K - 1):QK_HIST, :] = qb_in[0]
            c_out[...] = c_in[...]
            n_out[...] = n_in[...]
            m_out[...] = m_in[...]

    row_id = lax.broadcasted_iota(jnp.int32, (L, 1), 0) + c * L
    valid = row_id >= pad

    u = p_ref[0, :, OFF_A:OFF_A + D_CONV] * _sigmoid(p_ref[0, :, OFF_GL:OFF_GL + D_CONV])
    if pad:
        u = jnp.where(valid, u, 0.0)
    ubuf[CONV_HIST:CONV_HIST + L, :] = u
    base = CONV_HIST - (CONV_K - 1)
    for s in range(D_CONV // LANES):
        cs = slice(s * LANES, (s + 1) * LANES)
        acc = jnp.zeros((L, LANES), F32) + bdw[:, cs]
        for j in range(CONV_K):
            acc = acc + ubuf[base + j:base + j + L, cs] * wdw[j:j + 1, cs]
        cvo[:, cs] = acc
    cb_out[0] = ubuf[L + base:L + CONV_HIST, :]
    ubuf[0:CONV_HIST, :] = ubuf[L:L + CONV_HIST, :]

    cv = cvo[...]
    mu = jnp.mean(cv, axis=-1, keepdims=True)
    xc = cv - mu
    ln = xc * lax.rsqrt(jnp.mean(xc * xc, axis=-1, keepdims=True) + EPS) * lng[...] + lnb[...]
    ua = _silu(ln) * _silu(p_ref[0, :, OFF_ZC:OFF_ZC + D_CONV])
    br_c = jnp.dot(ua.astype(BF16), wco[...], preferred_element_type=F32)

    qk_pre = p_ref[0, :, OFF_Q:OFF_Q + 2 * D_ML]
    if pad:
        qk_pre = jnp.where(valid, qk_pre, 0.0)
    qkbuf[QK_HIST:QK_HIST + L, :] = qk_pre
    qbase = QK_HIST - (QK_CONV_K - 1)
    for s in range(2 * D_ML // (4 * LANES)):
        cs = slice(s * 4 * LANES, (s + 1) * 4 * LANES)
        acc = jnp.zeros((L, 4 * LANES), F32) + bqk[:, cs]
        for j in range(QK_CONV_K):
            acc = acc + qkbuf[qbase + j:qbase + j + L, cs] * wqk[j:j + 1, cs]
        acc = _silu(acc)
        if s * 4 * LANES >= D_ML:
            acc = acc * (ML_HEAD_DIM ** -0.5)
        qks[:, cs] = acc.astype(BF16)
    qb_out[0] = qkbuf[L + qbase:L + QK_HIST, :]
    qkbuf[0:QK_HIST, :] = qkbuf[L:L + QK_HIST, :]

    g = pg_ref[0]
    logf = jax.nn.log_sigmoid(g + fb[...])
    if pad:
        logf = jnp.where(valid, logf, 0.0)
    b_al = pltpu.roll(_cumsum_rows(logf, L), GATE_W - ML_HEADS, axis=1)
    m_prev = m_out[0]
    inter = b_al + m_prev
    d_src = g - b_al
    d_src_t = d_src.T
    tt = lax.broadcasted_iota(jnp.int32, (L, L), 0)
    ss = lax.broadcasted_iota(jnp.int32, (L, L), 1)
    mask = ss <= tt
    if pad:
        mask = mask & ((ss + c * L) >= pad)

    m_cols = []
    for h in range(ML_HEADS):
        hs = slice(h * ML_HEAD_DIM, (h + 1) * ML_HEAD_DIM)
        q = qks[:, hs]
        k = qks[:, D_ML + h * ML_HEAD_DIM:D_ML + (h + 1) * ML_HEAD_DIM]
        v = p_ref[0, :, OFF_V + h * ML_HEAD_DIM:OFF_V + (h + 1) * ML_HEAD_DIM]
        dmat = jnp.where(mask, b_al[:, h:h + 1] + d_src_t[h:h + 1, :], -jnp.inf)
        inter_h = inter[:, h:h + 1]
        m_t = jnp.maximum(inter_h, jnp.max(dmat, axis=-1, keepdims=True))
        w_intra = jnp.exp(dmat - m_t)
        w_inter = jnp.exp(inter_h - m_t)
        s_mat = lax.dot_general(q, k, (((1,), (1,)), ((), ())), preferred_element_type=F32) * w_intra
        c_old = c_out[0, h]
        n_old = n_out[0, h:h + 1, :]
        qc = lax.dot_general(q, c_old.astype(BF16), (((1,), (1,)), ((), ())), preferred_element_type=F32)
        num = jnp.dot(s_mat.astype(BF16), v.astype(BF16), preferred_element_type=F32) + w_inter * qc
        qn = jnp.sum(q.astype(F32) * n_old, axis=-1, keepdims=True)
        den = jnp.sum(s_mat, axis=-1, keepdims=True) + w_inter * qn
        hh = num * (1.0 / jnp.maximum(jnp.abs(den), jnp.exp(-m_t)))
        hh = hh * lax.rsqrt(jnp.mean(hh * hh, axis=-1, keepdims=True) + EPS) * mln[:, hs]
        hm = hh * _sigmoid(p_ref[0, :, OFF_O + h * ML_HEAD_DIM:OFF_O + (h + 1) * ML_HEAD_DIM]) \
            * _silu(p_ref[0, :, OFF_ZM + h * ML_HEAD_DIM:OFF_ZM + (h + 1) * ML_HEAD_DIM])
        hm_scr[:, hs] = hm.astype(BF16)

        m_new = m_t[L - 1:L, :]
        decay = jnp.exp(b_al[L - 1:L, h:h + 1] + m_prev[:, h:h + 1] - m_new)
        w_s = jnp.exp(d_src[:, h:h + 1] + (b_al[L - 1:L, h:h + 1] - m_new))
        if pad:
            w_s = jnp.where(valid, w_s, 0.0)
        vw = (w_s * v).astype(BF16)
        c_out[0, h] = decay * c_old + lax.dot_general(
            vw, k, (((0,), (0,)), ((), ())), preferred_element_type=F32)
        n_out[0, h:h + 1, :] = decay * n_old + jnp.sum(w_s * k.astype(F32), axis=0, keepdims=True)
        m_cols.append(m_new)

    lane = lax.broadcasted_iota(jnp.int32, (1, GATE_W), 1)
    m_row = jnp.zeros((1, GATE_W), F32)
    for h in range(ML_HEADS):
        m_row = jnp.where(lane == h, m_cols[h], m_row)
    m_out[0] = m_row

    br_m = jnp.dot(hm_scr[...], wmo[...], preferred_element_type=F32)
    ym = _sigmoid(p_ref[0, :, OFF_GC:OFF_GC + D_MODEL]) * br_c \
        + _sigmoid(p_ref[0, :, OFF_GM:OFF_GM + D_MODEL]) * br_m
    out = jnp.dot(ym.astype(BF16), wo[...], preferred_element_type=F32)
    y_ref[0] = x_ref[0] + out * lax.rsqrt(jnp.mean(out * out, axis=-1, keepdims=True) + EPS) * npost[...]


def _mixer(x, proj, gates, state, lw, L, pad):
    S, T, _ = x.shape
    fresh = state is None
    nc = T // L
    blk3 = lambda s, c: (s, c, 0)
    st3 = lambda s, c: (s, 0, 0)
    st4 = lambda s, c: (s, 0, 0, 0)
    w2 = lambda s, c: (0, 0)
    const = dict(pipeline_mode=pl.Buffered(1))

    in_specs = [
        pl.BlockSpec((1, L, D_MODEL), blk3),
        pl.BlockSpec((1, L, D_MAIN), blk3),
        pl.BlockSpec((1, L, GATE_W), blk3),
    ]
    args = [x, proj, gates]
    state_specs = [
        pl.BlockSpec((1, CONV_K - 1, D_CONV), st3),
        pl.BlockSpec((1, QK_CONV_K - 1, 2 * D_ML), st3),
        pl.BlockSpec((1, ML_HEADS, ML_HEAD_DIM, ML_HEAD_DIM), st4),
        pl.BlockSpec((1, ML_HEADS, ML_HEAD_DIM), st3),
        pl.BlockSpec((1, 1, GATE_W), st3),
    ]
    if not fresh:
        in_specs += state_specs
        args += list(state)
    weights = [lw["w_dw"], lw["b_dw"], lw["ln_g"], lw["ln_b"], lw["w_qkc"], lw["b_qkc"], lw["f_bias"],
               lw["ml_norm"], lw["w_conv_out"], lw["w_ml_out"], lw["w_out"], lw["norm_post"]]
    in_specs += [pl.BlockSpec(w.shape, w2, **const) for w in weights]
    args += weights

    out_shape = [
        jax.ShapeDtypeStruct((S, T, D_MODEL), F32),
        jax.ShapeDtypeStruct((S, CONV_K - 1, D_CONV), F32),
        jax.ShapeDtypeStruct((S, QK_CONV_K - 1, 2 * D_ML), F32),
        jax.ShapeDtypeStruct((S, ML_HEADS, ML_HEAD_DIM, ML_HEAD_DIM), F32),
        jax.ShapeDtypeStruct((S, ML_HEADS, ML_HEAD_DIM), F32),
        jax.ShapeDtypeStruct((S, 1, GATE_W), F32),
    ]
    out_specs = [pl.BlockSpec((1, L, D_MODEL), blk3)] + state_specs
    scratch = [
        pltpu.VMEM((L + CONV_HIST, D_CONV), F32),
        pltpu.VMEM((L, D_CONV), F32),
        pltpu.VMEM((L + QK_HIST, 2 * D_ML), F32),
        pltpu.VMEM((L, 2 * D_ML), BF16),
        pltpu.VMEM((L, D_ML), BF16),
    ]
    return pl.pallas_call(
        functools.partial(_mixer_kernel, L=L, pad=pad, fresh=fresh),
        grid=(S, nc),
        in_specs=in_specs,
        out_specs=out_specs,
        out_shape=out_shape,
        scratch_shapes=scratch,
        compiler_params=pltpu.CompilerParams(
            dimension_semantics=("arbitrary", "arbitrary"), vmem_limit_bytes=VMEM_LIMIT),
        name="mixer_fresh" if fresh else "mixer_carry",
    )(*args)


L_PROMPT = 128
TM_SAMPLE = 1024
TN_PROJ = 1024


def _layer_weights(l, norm_pre, norm_post, w_in, b_in, w_dw, b_dw, ln_g, ln_b, w_qk_conv, b_qk_conv,
                   f_bias, ml_norm, w_conv_out, w_ml_out, w_out):
    n_if = 2 * ML_HEADS
    if_lo = D_MAIN - 2 * D_MODEL
    w = w_in[l]
    b = b_in[l]
    w_main = jnp.concatenate([w[:, :if_lo], w[:, if_lo + n_if:]], axis=1).astype(BF16)
    b_main = jnp.concatenate([b[:if_lo], b[if_lo + n_if:]])[None, :]
    w_gate = jnp.pad(w[:, if_lo:if_lo + n_if], ((0, 0), (0, GATE_W - n_if))).astype(BF16)
    b_gate = jnp.pad(b[if_lo:if_lo + n_if], (0, GATE_W - n_if))[None, :]
    fb = jnp.pad(f_bias[l], (ML_HEADS, GATE_W - n_if))[None, :]
    return dict(
        norm_pre=norm_pre[l][None, :], norm_post=norm_post[l][None, :],
        w_main=w_main, b_main=b_main, w_gate=w_gate, b_gate=b_gate,
        w_dw=w_dw[l], b_dw=b_dw[l][None, :], ln_g=ln_g[l][None, :], ln_b=ln_b[l][None, :],
        w_qkc=w_qk_conv[l], b_qkc=b_qk_conv[l][None, :], f_bias=fb, ml_norm=ml_norm[l][None, :],
        w_conv_out=w_conv_out[l].astype(BF16), w_ml_out=w_ml_out[l].astype(BF16),
        w_out=w_out[l].astype(BF16))


def _run_layer(x, state, lw, L, pad, tm):
    S, T, _ = x.shape
    proj, gates = _inproj(x.reshape(S * T, D_MODEL), lw["norm_pre"], lw["w_main"], lw["b_main"],
                          lw["w_gate"], lw["b_gate"], tm, TN_PROJ)
    return _mixer(x, proj.reshape(S, T, D_MAIN), gates.reshape(S, T, GATE_W), state, lw, L, pad)


def kernel(x_prompt, x_sample, state_conv, state_qk_conv, state_C, state_n, state_m, meta_tokens,
           norm_pre, norm_post, w_in, b_in, w_dw, b_dw, ln_g, ln_b, w_qk_conv, b_qk_conv, f_bias,
           ml_norm, w_conv_out, w_ml_out, w_out):
    bp, seq, _ = x_prompt.shape
    bs, dec_seq, _ = x_sample.shape
    depth = w_in.shape[0]
    t_real = N_META + seq
    t_pad = -(-t_real // L_PROMPT) * L_PROMPT
    pad = t_pad - t_real
    n_tok_p = bp * t_pad
    tm_p = max(t for t in range(SUBLANES, 1537, SUBLANES) if n_tok_p % t == 0)

    meta = jnp.broadcast_to(meta_tokens.astype(x_prompt.dtype)[None], (bp, N_META, D_MODEL))
    xp = jnp.concatenate([jnp.zeros((bp, pad, D_MODEL), x_prompt.dtype), meta, x_prompt], axis=1)
    xs = x_sample
    m_pad = jnp.pad(state_m, ((0, 0), (0, 0), (0, GATE_W - ML_HEADS)))[:, :, None, :]

    outs_p, outs_s = [], []
    for l in range(depth):
        lw = _layer_weights(l, norm_pre, norm_post, w_in, b_in, w_dw, b_dw, ln_g, ln_b, w_qk_conv,
                            b_qk_conv, f_bias, ml_norm, w_conv_out, w_ml_out, w_out)
        xp, *st_p = _run_layer(xp, None, lw, L_PROMPT, pad, tm_p)
        outs_p.append(st_p)
        st_in = (state_conv[l], state_qk_conv[l], state_C[l], state_n[l], m_pad[l])
        xs, *st_s = _run_layer(xs, st_in, lw, dec_seq, 0, TM_SAMPLE)
        outs_s.append(st_s)

    def stack(outs, k):
        a = jnp.stack([o[k] for o in outs])
        return a[:, :, 0, :ML_HEADS] if k == 4 else a

    return (xp[:, pad + N_META:], xs,
            stack(outs_p, 0), stack(outs_p, 1), stack(outs_p, 2), stack(outs_p, 3), stack(outs_p, 4),
            stack(outs_s, 0), stack(outs_s, 1), stack(outs_s, 2), stack(outs_s, 3), stack(outs_s, 4))
```

```python
import functools

import jax
import jax.numpy as jnp
from jax import lax
from jax.experimental import pallas as pl
from jax.experimental.pallas import tpu as pltpu

F32 = jnp.float32
BF16 = jnp.bfloat16

D_MODEL = 1024
D_CONV = D_MODEL
CONV_K = 31
D_ML = 2 * D_MODEL
ML_HEADS = 4
ML_HEAD_DIM = D_ML // ML_HEADS
QK_CONV_K = 4
N_META = 16
EPS = 1e-6

LANES = 128
SUBLANES = 8
D_MAIN = 3 * D_CONV + 5 * D_ML + 2 * D_MODEL
GATE_W = LANES
CONV_HIST = 32
QK_HIST = SUBLANES
VMEM_LIMIT = 62 * 1024 * 1024

OFF_A, OFF_GL, OFF_ZC = 0, D_CONV, 2 * D_CONV
OFF_Q = 3 * D_CONV
OFF_K = OFF_Q + D_ML
OFF_V = OFF_K + D_ML
OFF_O = OFF_V + D_ML
OFF_ZM = OFF_O + D_ML
OFF_GC = OFF_ZM + D_ML
OFF_GM = OFF_GC + D_MODEL

N_STATE = 5


def _sigmoid(x):
    return jax.nn.sigmoid(x)


def _silu(x):
    return x * jax.nn.sigmoid(x)


def _rms(x, g):
    return x * lax.rsqrt(jnp.mean(x * x, axis=-1, keepdims=True) + EPS) * g


def _inproj_kernel(x_ref, g_ref, w_ref, b_ref, wg_ref, bg_ref, o_ref, og_ref, h_ref):
    @pl.when(pl.program_id(1) == 0)
    def _():
        hb = _rms(x_ref[...], g_ref[...]).astype(BF16)
        h_ref[...] = hb
        og_ref[...] = jnp.dot(hb, wg_ref[...], preferred_element_type=F32) + bg_ref[...]

    o_ref[...] = jnp.dot(h_ref[...], w_ref[...], preferred_element_type=F32) + b_ref[...]


def _inproj(x2d, g, w_main, b_main, w_gate, b_gate, tm, tn):
    n_tok = x2d.shape[0]
    grid = (n_tok // tm, D_MAIN // tn)
    return pl.pallas_call(
        _inproj_kernel,
        grid=grid,
        in_specs=[
            pl.BlockSpec((tm, D_MODEL), lambda i, j: (i, 0)),
            pl.BlockSpec((1, D_MODEL), lambda i, j: (0, 0)),
            pl.BlockSpec((D_MODEL, tn), lambda i, j: (0, j)),
            pl.BlockSpec((1, tn), lambda i, j: (0, j)),
            pl.BlockSpec((D_MODEL, GATE_W), lambda i, j: (0, 0)),
            pl.BlockSpec((1, GATE_W), lambda i, j: (0, 0)),
        ],
        out_specs=[
            pl.BlockSpec((tm, tn), lambda i, j: (i, j)),
            pl.BlockSpec((tm, GATE_W), lambda i, j: (i, 0)),
        ],
        out_shape=[
            jax.ShapeDtypeStruct((n_tok, D_MAIN), F32),
            jax.ShapeDtypeStruct((n_tok, GATE_W), F32),
        ],
        scratch_shapes=[pltpu.VMEM((tm, D_MODEL), BF16)],
        compiler_params=pltpu.CompilerParams(
            dimension_semantics=("arbitrary", "arbitrary"), vmem_limit_bytes=VMEM_LIMIT),
        name="inproj",
    )(x2d, g, w_main, b_main, w_gate, b_gate)


def _cumsum_rows(x, n_rows):
    row = lax.broadcasted_iota(jnp.int32, x.shape, 0)
    k = 1
    while k < n_rows:
        x = x + jnp.where(row >= k, pltpu.roll(x, k, axis=0), 0.0)
        k *= 2
    return x


def _mixer_kernel(*refs, L, pad, fresh, first_block, chained):
    refs = list(refs)
    x_ref = refs.pop(0)
    x0_ref = refs.pop(0) if first_block else None
    if fresh:
        npre, wmain, bmain, wgate, bgate = (refs.pop(0) for _ in range(5))
    else:
        p_ref, pg_ref = refs.pop(0), refs.pop(0)
        cb_in, qb_in, c_in, n_in, m_in = (refs.pop(0) for _ in range(N_STATE))
    wdw, bdw, lng, lnb, wqk, bqk, fb, mln, wco, wmo, wo, npost = (refs.pop(0) for _ in range(12))
    if chained:
        del refs[:N_STATE]
    y_ref, cb_out, qb_out, c_out, n_out, m_out = (refs.pop(0) for _ in range(1 + N_STATE))
    ubuf, cvo, qkbuf, qks, hm_scr = (refs.pop(0) for _ in range(5))
    h_scr = refs.pop(0) if fresh else None

    c = pl.program_id(1)

    @pl.when(c == 0)
    def _init():
        ubuf[0:CONV_HIST, :] = jnp.zeros((CONV_HIST, D_CONV), F32)
        qkbuf[0:QK_HIST, :] = jnp.zeros((QK_HIST, 2 * D_ML), F32)
        if fresh:
            c_out[...] = jnp.zeros(c_out.shape, F32)
            n_out[...] = jnp.zeros(n_out.shape, F32)
            m_out[...] = jnp.zeros(m_out.shape, F32)
        else:
            ubuf[CONV_HIST - (CONV_K - 1):CONV_HIST, :] = cb_in[0]
            qkbuf[QK_HIST - (QK_CONV_K - 1):QK_HIST, :] = qb_in[0]
            c_out[...] = c_in[...]
            n_out[...] = n_in[...]
            m_out[...] = m_in[...]

    def x_block():
        if first_block:
            return jnp.where(c == 0, x0_ref[...], x_ref[0])
        return x_ref[0]

    if fresh:
        h_scr[...] = _rms(x_block(), npre[...]).astype(BF16)

        def proj(off, width):
            return jnp.dot(h_scr[...], wmain[:, off:off + width], preferred_element_type=F32) \
                + bmain[:, off:off + width]

        g = jnp.dot(h_scr[...], wgate[...], preferred_element_type=F32) + bgate[...]
    else:
        def proj(off, width):
            return p_ref[0, :, off:off + width]

        g = pg_ref[0]

    row_id = lax.broadcasted_iota(jnp.int32, (L, 1), 0) + c * L
    valid = row_id >= pad

    u = proj(OFF_A, D_CONV) * _sigmoid(proj(OFF_GL, D_CONV))
    if pad:
        u = jnp.where(valid, u, 0.0)
    ubuf[CONV_HIST:CONV_HIST + L, :] = u
    base = CONV_HIST - (CONV_K - 1)
    for s in range(D_CONV // LANES):
        cs = slice(s * LANES, (s + 1) * LANES)
        acc = jnp.zeros((L, LANES), F32) + bdw[:, cs]
        for j in range(CONV_K):
            acc = acc + ubuf[base + j:base + j + L, cs] * wdw[j:j + 1, cs]
        cvo[:, cs] = acc
    cb_out[0] = ubuf[L + base:L + CONV_HIST, :]
    ubuf[0:CONV_HIST, :] = ubuf[L:L + CONV_HIST, :]

    cv = cvo[...]
    mu = jnp.mean(cv, axis=-1, keepdims=True)
    xc = cv - mu
    ln = xc * lax.rsqrt(jnp.mean(xc * xc, axis=-1, keepdims=True) + EPS) * lng[...] + lnb[...]
    ua = _silu(ln) * _silu(proj(OFF_ZC, D_CONV))
    br_c = jnp.dot(ua.astype(BF16), wco[...], preferred_element_type=F32)

    qw = 4 * LANES
    qbase = QK_HIST - (QK_CONV_K - 1)
    for s in range(2 * D_ML // qw):
        cs = slice(s * qw, (s + 1) * qw)
        qk_pre = proj(OFF_Q + s * qw, qw)
        if pad:
            qk_pre = jnp.where(valid, qk_pre, 0.0)
        qkbuf[QK_HIST:QK_HIST + L, cs] = qk_pre
        acc = qk_pre * wqk[QK_CONV_K - 1:QK_CONV_K, cs] + bqk[:, cs]
        for j in range(QK_CONV_K - 1):
            acc = acc + qkbuf[qbase + j:qbase + j + L, cs] * wqk[j:j + 1, cs]
        acc = _silu(acc)
        if s * qw >= D_ML:
            acc = acc * (ML_HEAD_DIM ** -0.5)
        qks[:, cs] = acc.astype(BF16)
    qb_out[0] = qkbuf[L + qbase:L + QK_HIST, :]
    qkbuf[0:QK_HIST, :] = qkbuf[L:L + QK_HIST, :]

    logf = jax.nn.log_sigmoid(g + fb[...])
    if pad:
        logf = jnp.where(valid, logf, 0.0)
    b_al = pltpu.roll(_cumsum_rows(logf, L), GATE_W - ML_HEADS, axis=1)
    m_prev = m_out[0]
    inter = b_al + m_prev
    d_src = g - b_al
    d_src_t = d_src.T
    tt = lax.broadcasted_iota(jnp.int32, (L, L), 0)
    ss = lax.broadcasted_iota(jnp.int32, (L, L), 1)
    mask = ss <= tt
    if pad:
        mask = mask & ((ss + c * L) >= pad)

    m_cols = []
    for h in range(ML_HEADS):
        hs = slice(h * ML_HEAD_DIM, (h + 1) * ML_HEAD_DIM)
        q = qks[:, hs]
        k = qks[:, D_ML + h * ML_HEAD_DIM:D_ML + (h + 1) * ML_HEAD_DIM]
        v = proj(OFF_V + h * ML_HEAD_DIM, ML_HEAD_DIM)
        dmat = jnp.where(mask, b_al[:, h:h + 1] + d_src_t[h:h + 1, :], -jnp.inf)
        inter_h = inter[:, h:h + 1]
        m_t = jnp.maximum(inter_h, jnp.max(dmat, axis=-1, keepdims=True))
        w_intra = jnp.exp(dmat - m_t)
        w_inter = jnp.exp(inter_h - m_t)
        s_mat = lax.dot_general(q, k, (((1,), (1,)), ((), ())), preferred_element_type=F32) * w_intra
        c_old = c_out[0, h]
        n_old = n_out[0, h:h + 1, :]
        qc = lax.dot_general(q, c_old.astype(BF16), (((1,), (1,)), ((), ())), preferred_element_type=F32)
        num = jnp.dot(s_mat.astype(BF16), v.astype(BF16), preferred_element_type=F32) + w_inter * qc
        qn = jnp.sum(q.astype(F32) * n_old, axis=-1, keepdims=True)
        den = jnp.sum(s_mat, axis=-1, keepdims=True) + w_inter * qn
        hh = num * (1.0 / jnp.maximum(jnp.abs(den), jnp.exp(-m_t)))
        hh = hh * lax.rsqrt(jnp.mean(hh * hh, axis=-1, keepdims=True) + EPS) * mln[:, hs]
        hm = hh * _sigmoid(proj(OFF_O + h * ML_HEAD_DIM, ML_HEAD_DIM)) \
            * _silu(proj(OFF_ZM + h * ML_HEAD_DIM, ML_HEAD_DIM))
        hm_scr[:, hs] = hm.astype(BF16)

        m_new = m_t[L - 1:L, :]
        decay = jnp.exp(b_al[L - 1:L, h:h + 1] + m_prev[:, h:h + 1] - m_new)
        w_s = jnp.exp(d_src[:, h:h + 1] + (b_al[L - 1:L, h:h + 1] - m_new))
        if pad:
            w_s = jnp.where(valid, w_s, 0.0)
        vw = (w_s * v).astype(BF16)
        c_out[0, h] = decay * c_old + lax.dot_general(
            vw, k, (((0,), (0,)), ((), ())), preferred_element_type=F32)
        n_out[0, h:h + 1, :] = decay * n_old + jnp.sum(w_s * k.astype(F32), axis=0, keepdims=True)
        m_cols.append(m_new)

    lane = lax.broadcasted_iota(jnp.int32, (1, GATE_W), 1)
    m_row = jnp.zeros((1, GATE_W), F32)
    for h in range(ML_HEADS):
        m_row = jnp.where(lane == h, m_cols[h], m_row)
    m_out[0] = m_row

    br_m = jnp.dot(hm_scr[...], wmo[...], preferred_element_type=F32)
    ym = _sigmoid(proj(OFF_GC, D_MODEL)) * br_c + _sigmoid(proj(OFF_GM, D_MODEL)) * br_m
    out = jnp.dot(ym.astype(BF16), wo[...], preferred_element_type=F32)
    y_ref[0] = x_block() + _rms(out, npost[...])


def _state_shapes(depth, S):
    return [
        (depth, S, CONV_K - 1, D_CONV),
        (depth, S, QK_CONV_K - 1, 2 * D_ML),
        (depth, S, ML_HEADS, ML_HEAD_DIM, ML_HEAD_DIM),
        (depth, S, ML_HEADS, ML_HEAD_DIM),
        (depth, S, 1, GATE_W),
    ]


def _mixer(x, lw, layer, depth, L, *, x0=None, proj=None, state=None, prev_out=None, pad=0,
           drop_first=False):
    fresh = state is None
    first_block = x0 is not None
    chained = prev_out is not None
    S = x.shape[0]
    T = x.shape[1] + (L if first_block else 0)
    nc = T // L
    shapes = _state_shapes(depth, S)

    shift = lambda c: jnp.maximum(c - 1, 0)
    full2 = lambda s, c: (0, 0)
    const = dict(pipeline_mode=pl.Buffered(1))

    def state_spec(shape):
        nd = len(shape) - 2
        return pl.BlockSpec((None, 1) + shape[2:], lambda s, c: (layer, s) + (0,) * nd)

    in_specs = [pl.BlockSpec((1, L, D_MODEL), (lambda s, c: (s, shift(c), 0)) if first_block
                             else (lambda s, c: (s, c, 0)))]
    args = [x]
    if first_block:
        in_specs.append(pl.BlockSpec((L, D_MODEL), full2, **const))
        args.append(x0)
    if fresh:
        ws = [lw["norm_pre"], lw["w_main"], lw["b_main"], lw["w_gate"], lw["b_gate"]]
        in_specs += [pl.BlockSpec(w.shape, full2, **const) for w in ws]
        args += ws
    else:
        in_specs += [pl.BlockSpec((1, L, D_MAIN), lambda s, c: (s, c, 0)),
                     pl.BlockSpec((1, L, GATE_W), lambda s, c: (s, c, 0))]
        args += list(proj)
        in_specs += [state_spec(sh) for sh in shapes]
        args += list(state)
    ws = [lw["w_dw"], lw["b_dw"], lw["ln_g"], lw["ln_b"], lw["w_qkc"], lw["b_qkc"], lw["f_bias"],
          lw["ml_norm"], lw["w_conv_out"], lw["w_ml_out"], lw["w_out"], lw["norm_post"]]
    in_specs += [pl.BlockSpec(w.shape, full2, **const) for w in ws]
    args += ws
    aliases = {}
    if chained:
        for k, a in enumerate(prev_out):
            aliases[len(args)] = 1 + k
            in_specs.append(pl.BlockSpec(memory_space=pl.ANY))
            args.append(a)

    t_out = T - L if drop_first else T
    out_shape = [jax.ShapeDtypeStruct((S, t_out, D_MODEL), F32)] + \
        [jax.ShapeDtypeStruct(sh, F32) for sh in shapes]
    out_specs = [pl.BlockSpec((1, L, D_MODEL), (lambda s, c: (s, shift(c), 0)) if drop_first
                              else (lambda s, c: (s, c, 0)))] + [state_spec(sh) for sh in shapes]
    scratch = [
        pltpu.VMEM((L + CONV_HIST, D_CONV), F32),
        pltpu.VMEM((L, D_CONV), F32),
        pltpu.VMEM((L + QK_HIST, 2 * D_ML), F32),
        pltpu.VMEM((L, 2 * D_ML), BF16),
        pltpu.VMEM((L, D_ML), BF16),
    ]
    if fresh:
        scratch.append(pltpu.VMEM((L, D_MODEL), BF16))
    outs = pl.pallas_call(
        functools.partial(_mixer_kernel, L=L, pad=pad, fresh=fresh, first_block=first_block,
                          chained=chained),
        grid=(S, nc),
        in_specs=in_specs,
        out_specs=out_specs,
        out_shape=out_shape,
        scratch_shapes=scratch,
        input_output_aliases=aliases,
        compiler_params=pltpu.CompilerParams(
            dimension_semantics=("arbitrary", "arbitrary"), vmem_limit_bytes=VMEM_LIMIT),
        name="mixer_fresh" if fresh else "mixer_carry",
    )(*args)
    return outs[0], outs[1:]


L_PROMPT = 128
TM_SAMPLE = 1024
TN_PROJ = 1024


def _layer_weights(l, norm_pre, norm_post, w_in, b_in, w_dw, b_dw, ln_g, ln_b, w_qk_conv, b_qk_conv,
                   f_bias, ml_norm, w_conv_out, w_ml_out, w_out):
    n_if = 2 * ML_HEADS
    if_lo = D_MAIN - 2 * D_MODEL
    w = w_in[l]
    b = b_in[l]
    w_main = jnp.concatenate([w[:, :if_lo], w[:, if_lo + n_if:]], axis=1).astype(BF16)
    b_main = jnp.concatenate([b[:if_lo], b[if_lo + n_if:]])[None, :]
    w_gate = jnp.pad(w[:, if_lo:if_lo + n_if], ((0, 0), (0, GATE_W - n_if))).astype(BF16)
    b_gate = jnp.pad(b[if_lo:if_lo + n_if], (0, GATE_W - n_if))[None, :]
    fb = jnp.pad(f_bias[l], (ML_HEADS, GATE_W - n_if))[None, :]
    return dict(
        norm_pre=norm_pre[l][None, :], norm_post=norm_post[l][None, :],
        w_main=w_main, b_main=b_main, w_gate=w_gate, b_gate=b_gate,
        w_dw=w_dw[l], b_dw=b_dw[l][None, :], ln_g=ln_g[l][None, :], ln_b=ln_b[l][None, :],
        w_qkc=w_qk_conv[l], b_qkc=b_qk_conv[l][None, :], f_bias=fb, ml_norm=ml_norm[l][None, :],
        w_conv_out=w_conv_out[l].astype(BF16), w_ml_out=w_ml_out[l].astype(BF16),
        w_out=w_out[l].astype(BF16))


def kernel(x_prompt, x_sample, state_conv, state_qk_conv, state_C, state_n, state_m, meta_tokens,
           norm_pre, norm_post, w_in, b_in, w_dw, b_dw, ln_g, ln_b, w_qk_conv, b_qk_conv, f_bias,
           ml_norm, w_conv_out, w_ml_out, w_out):
    bp, seq, _ = x_prompt.shape
    bs, dec_seq, _ = x_sample.shape
    depth = w_in.shape[0]
    L = L_PROMPT
    pad = L - N_META
    assert seq % L == 0 and 0 <= pad

    x0 = jnp.concatenate([jnp.zeros((pad, D_MODEL), x_prompt.dtype), meta_tokens.astype(x_prompt.dtype)])
    m_pad = jnp.pad(state_m, ((0, 0), (0, 0), (0, GATE_W - ML_HEADS)))[:, :, None, :]
    state_in = (state_conv, state_qk_conv, state_C, state_n, m_pad)

    xp, xs = x_prompt, x_sample
    st_p = st_s = None
    for l in range(depth):
        lw = _layer_weights(l, norm_pre, norm_post, w_in, b_in, w_dw, b_dw, ln_g, ln_b, w_qk_conv,
                            b_qk_conv, f_bias, ml_norm, w_conv_out, w_ml_out, w_out)
        xp, st_p = _mixer(xp, lw, l, depth, L, x0=x0 if l == 0 else None, prev_out=st_p, pad=pad,
                          drop_first=(l == depth - 1))
        pm, pg = _inproj(xs.reshape(bs * dec_seq, D_MODEL), lw["norm_pre"], lw["w_main"], lw["b_main"],
                         lw["w_gate"], lw["b_gate"], TM_SAMPLE, TN_PROJ)
        xs, st_s = _mixer(xs, lw, l, depth, dec_seq,
                          proj=(pm.reshape(bs, dec_seq, D_MAIN), pg.reshape(bs, dec_seq, GATE_W)),
                          state=state_in, prev_out=st_s)

    fin = lambda st: tuple(st[:4]) + (st[4][:, :, 0, :ML_HEADS],)
    return (xp, xs) + fin(st_p) + fin(st_s)
```

```python
import functools

import jax
import jax.numpy as jnp
from jax import lax
from jax.experimental import pallas as pl
from jax.experimental.pallas import tpu as pltpu

F32 = jnp.float32
BF16 = jnp.bfloat16
U32 = jnp.uint32

D_MODEL = 1024
D_CONV = D_MODEL
CONV_K = 31
D_ML = 2 * D_MODEL
ML_HEADS = 4
ML_HEAD_DIM = D_ML // ML_HEADS
QK_CONV_K = 4
N_META = 16
EPS = 1e-6

LANES = 128
SUBLANES = 8
GATE_W = LANES
CONV_HIST = 32
CONV_BASE = CONV_HIST - (CONV_K - 1)
CONV_SPAN = (CONV_BASE + CONV_K - 2) // SUBLANES * SUBLANES
QK_HIST = SUBLANES
QK_BASE = QK_HIST - (QK_CONV_K - 1)
QK_CHUNK = ML_HEAD_DIM
CONV_ROWS = 128
PACK_TN = 1024
VMEM_LIMIT = 60 * 1024 * 1024

CA_A, CA_GL, CA_ZC, CA_GC = 0, D_CONV, 2 * D_CONV, 3 * D_CONV
D_PA = 3 * D_CONV + D_MODEL
CB_Q, CB_K, CB_V, CB_O, CB_ZM, CB_GM = 0, D_ML, 2 * D_ML, 3 * D_ML, 4 * D_ML, 5 * D_ML
D_PB = 5 * D_ML + D_MODEL


def _sigmoid(x):
    return jax.nn.sigmoid(x)


def _silu(x):
    return x * jax.nn.sigmoid(x)


def _rms(x, g):
    return x * lax.rsqrt(jnp.mean(x * x, axis=-1, keepdims=True) + EPS) * g


def _words(x_bf16):
    return pltpu.bitcast(x_bf16, U32)


def _halves(x_u32):
    return pltpu.bitcast(x_u32, BF16)


def _wdot(x_bf16, w_words):
    return jnp.dot(x_bf16, _halves(w_words), preferred_element_type=F32)


def _params(**kw):
    return pltpu.CompilerParams(dimension_semantics=("arbitrary", "arbitrary"),
                                vmem_limit_bytes=VMEM_LIMIT, **kw)


def _pack_kernel(w_ref, o_ref):
    o_ref[...] = _words(w_ref[...])


def _pack(w):
    k, n = w.shape
    tn = min(n, PACK_TN)
    return pl.pallas_call(
        _pack_kernel,
        grid=(n // tn,),
        in_specs=[pl.BlockSpec((k, tn), lambda j: (0, j))],
        out_specs=pl.BlockSpec((k // 2, tn), lambda j: (0, j)),
        out_shape=jax.ShapeDtypeStruct((k // 2, n), U32),
        name="pack_weight",
    )(w)


def _inproj_kernel(*refs, gated):
    if gated:
        x_ref, g_ref, w_ref, b_ref, wg_ref, bg_ref, o_ref, og_ref, h_ref = refs
    else:
        x_ref, g_ref, w_ref, b_ref, o_ref, h_ref = refs

    @pl.when(pl.program_id(1) == 0)
    def _():
        hb = _rms(x_ref[...], g_ref[...]).astype(BF16)
        h_ref[...] = hb
        if gated:
            og_ref[...] = jnp.dot(hb, wg_ref[...], preferred_element_type=F32) + bg_ref[...]

    o_ref[...] = jnp.dot(h_ref[...], w_ref[...], preferred_element_type=F32) + b_ref[...]


def _inproj(x2d, g, w, b, tm, tn, w_gate=None, b_gate=None):
    n_tok, n_col = x2d.shape[0], w.shape[1]
    gated = w_gate is not None
    in_specs = [
        pl.BlockSpec((tm, D_MODEL), lambda i, j: (i, 0)),
        pl.BlockSpec((1, D_MODEL), lambda i, j: (0, 0)),
        pl.BlockSpec((D_MODEL, tn), lambda i, j: (0, j)),
        pl.BlockSpec((1, tn), lambda i, j: (0, j)),
    ]
    out_specs = [pl.BlockSpec((tm, tn), lambda i, j: (i, j))]
    out_shape = [jax.ShapeDtypeStruct((n_tok, n_col), F32)]
    args = [x2d, g, w, b]
    if gated:
        in_specs += [pl.BlockSpec((D_MODEL, GATE_W), lambda i, j: (0, 0)),
                     pl.BlockSpec((1, GATE_W), lambda i, j: (0, 0))]
        out_specs.append(pl.BlockSpec((tm, GATE_W), lambda i, j: (i, 0)))
        out_shape.append(jax.ShapeDtypeStruct((n_tok, GATE_W), F32))
        args += [w_gate, b_gate]
    return pl.pallas_call(
        functools.partial(_inproj_kernel, gated=gated),
        grid=(n_tok // tm, n_col // tn),
        in_specs=in_specs, out_specs=out_specs, out_shape=out_shape,
        scratch_shapes=[pltpu.VMEM((tm, D_MODEL), BF16)],
        compiler_params=_params(),
        name="inproj",
    )(*args)


def _cumsum_rows(x, n_rows):
    row = lax.broadcasted_iota(jnp.int32, x.shape, 0)
    k = 1
    while k < n_rows:
        x = x + jnp.where(row >= k, pltpu.roll(x, k, axis=0), 0.0)
        k *= 2
    return x


def _conv_kernel(*refs, L, pad, fresh, first_block, chained):
    refs = list(refs)
    x_ref = refs.pop(0) if fresh else None
    meta_ref = refs.pop(0) if first_block else None
    if fresh:
        npre, wpa, bpa = (refs.pop(0) for _ in range(3))
    else:
        p_ref, cb_in = refs.pop(0), refs.pop(0)
    wdw, bdw, lng, lnb, wco = (refs.pop(0) for _ in range(5))
    if chained:
        refs.pop(0)
    gcb_ref, cb_out = refs.pop(0), refs.pop(0)
    ubuf, ush, cvo, zg = (refs.pop(0) for _ in range(4))

    c = pl.program_id(1)

    @pl.when(c == 0)
    def _init():
        ubuf[0:CONV_HIST, :] = jnp.zeros((CONV_HIST, D_CONV), F32)
        if not fresh:
            ubuf[CONV_BASE:CONV_HIST, :] = cb_in[0]

    if fresh:
        x = x_ref[0]
        if first_block:
            x0 = jnp.concatenate([jnp.zeros((pad, D_MODEL), F32), meta_ref[...]], axis=0)
            x = jnp.where(c == 0, x0, x)
        h = _rms(x, npre[...]).astype(BF16)

        def proj(off, width):
            return _wdot(h, wpa[:, off:off + width]) + bpa[:, off:off + width]
    else:
        def proj(off, width):
            return p_ref[0, :, off:off + width]

    u = proj(CA_A, D_CONV) * _sigmoid(proj(CA_GL, D_CONV))
    if pad:
        row_id = lax.broadcasted_iota(jnp.int32, (L, 1), 0) + c * L
        u = jnp.where(row_id >= pad, u, 0.0)
    ubuf[CONV_HIST:CONV_HIST + L, :] = u

    rc = min(L, CONV_ROWS)
    n_strip = D_CONV // LANES
    zw = 2 * D_CONV // n_strip
    for it in range(n_strip):
        zg[:, it * zw:(it + 1) * zw] = proj(CA_ZC + it * zw, zw)
        ls = slice(it * LANES, (it + 1) * LANES)
        for sh in range(1, SUBLANES):
            ush[sh - 1, :, :] = ubuf[sh:sh + L + CONV_SPAN, ls]
        for r0 in range(0, L, rc):
            acc = jnp.zeros((rc, LANES), F32) + bdw[:, ls]
            for j in range(CONV_K):
                a8, sh = divmod(CONV_BASE + j, SUBLANES)
                lo = a8 * SUBLANES + r0
                src = ubuf[lo:lo + rc, ls] if sh == 0 else ush[sh - 1, lo:lo + rc, :]
                acc = acc + src * wdw[j:j + 1, ls]
            cvo[r0:r0 + rc, ls] = acc
    cb_out[0] = ubuf[L + CONV_BASE:L + CONV_HIST, :]
    ubuf[0:CONV_HIST, :] = ubuf[L:L + CONV_HIST, :]

    cv = cvo[...]
    mu = jnp.mean(cv, axis=-1, keepdims=True)
    xc = cv - mu
    ln = xc * lax.rsqrt(jnp.mean(xc * xc, axis=-1, keepdims=True) + EPS) * lng[...] + lnb[...]
    ua = _silu(ln) * _silu(zg[:, 0:D_CONV])
    br_c = _wdot(ua.astype(BF16), wco[...])
    gcb_ref[0] = _sigmoid(zg[:, D_CONV:2 * D_CONV]) * br_c


def _conv_branch(x, lw, layer, depth, L, *, meta=None, proj=None, state=None, prev_out=None, pad=0):
    fresh = proj is None
    first_block = meta is not None
    chained = prev_out is not None
    src = x if fresh else proj
    S = src.shape[0]
    T = src.shape[1] + (L if first_block else 0)
    shift = lambda c: jnp.maximum(c - 1, 0)
    full2 = lambda s, c: (0, 0)
    const = dict(pipeline_mode=pl.Buffered(1))
    st_shape = (depth, S, CONV_K - 1, D_CONV)
    st_spec = pl.BlockSpec((None, 1) + st_shape[2:], lambda s, c: (layer, s, 0, 0))

    in_specs, args = [], []
    if fresh:
        in_specs.append(pl.BlockSpec((1, L, D_MODEL), (lambda s, c: (s, shift(c), 0)) if first_block
                                     else (lambda s, c: (s, c, 0))))
        args.append(x)
        if first_block:
            in_specs.append(pl.BlockSpec(meta.shape, full2, **const))
            args.append(meta)
        ws = [lw["norm_pre"], lw["w_pa_words"], lw["b_pa"]]
    else:
        in_specs += [pl.BlockSpec((1, L, D_PA), lambda s, c: (s, c, 0)), st_spec]
        args += [proj, state]
        ws = []
    ws += [lw["w_dw"], lw["b_dw"], lw["ln_g"], lw["ln_b"], lw["w_conv_out_words"]]
    in_specs += [pl.BlockSpec(w.shape, full2, **const) for w in ws]
    args += ws
    aliases = {}
    if chained:
        aliases[len(args)] = 1
        in_specs.append(pl.BlockSpec(memory_space=pl.ANY))
        args.append(prev_out)

    scratch = [
        pltpu.VMEM((L + CONV_HIST, D_CONV), F32),
        pltpu.VMEM((SUBLANES - 1, L + CONV_SPAN, LANES), F32),
        pltpu.VMEM((L, D_CONV), F32),
        pltpu.VMEM((L, 2 * D_CONV), F32),
    ]
    return pl.pallas_call(
        functools.partial(_conv_kernel, L=L, pad=pad, fresh=fresh, first_block=first_block,
                          chained=chained),
        grid=(S, T // L),
        in_specs=in_specs,
        out_specs=[pl.BlockSpec((1, L, D_MODEL), lambda s, c: (s, c, 0)), st_spec],
        out_shape=[jax.ShapeDtypeStruct((S, T, D_MODEL), F32), jax.ShapeDtypeStruct(st_shape, F32)],
        scratch_shapes=scratch,
        input_output_aliases=aliases,
        compiler_params=_params(),
        name="conv_fresh" if fresh else "conv_carry",
    )(*args)


N_MSTATE = 4


def _mlstm_kernel(*refs, L, pad, fresh, first_block, chained):
    refs = list(refs)
    x_ref = refs.pop(0)
    meta_ref = refs.pop(0) if first_block else None
    gcb_ref = refs.pop(0)
    if fresh:
        npre, wpb, bpb, wgate, bgate = (refs.pop(0) for _ in range(5))
    else:
        p_ref, pg_ref = refs.pop(0), refs.pop(0)
        qb_in, c_in, n_in, m_in = (refs.pop(0) for _ in range(N_MSTATE))
    wqk, bqk, fb, mln, wmo, wo, npost = (refs.pop(0) for _ in range(7))
    if chained:
        del refs[:N_MSTATE]
    y_ref, qb_out, c_out, n_out, m_out = (refs.pop(0) for _ in range(1 + N_MSTATE))
    qkc, qkh, qks, hm_scr = (refs.pop(0) for _ in range(4))

    c = pl.program_id(1)

    @pl.when(c == 0)
    def _init():
        qkh[...] = jnp.zeros(qkh.shape, F32)
        if fresh:
            c_out[...] = jnp.zeros(c_out.shape, F32)
            n_out[...] = jnp.zeros(n_out.shape, F32)
            m_out[...] = jnp.zeros(m_out.shape, F32)
        else:
            qkh[QK_BASE:QK_HIST, :] = qb_in[0]
            c_out[...] = c_in[...]
            n_out[...] = n_in[...]
            m_out[...] = m_in[...]

    y_ref[0] = x_ref[0]
    if first_block:
        @pl.when(c == 0)
        def _first():
            y_ref[0, 0:pad, :] = jnp.zeros((pad, D_MODEL), F32)
            y_ref[0, pad:L, :] = meta_ref[...]

    if fresh:
        h = _rms(y_ref[0], npre[...]).astype(BF16)

        def proj(off, width):
            return _wdot(h, wpb[:, off:off + width]) + bpb[:, off:off + width]

        g = _wdot(h, wgate[...]) + bgate[...]
    else:
        def proj(off, width):
            return p_ref[0, :, off:off + width]

        g = pg_ref[0]

    row_id = lax.broadcasted_iota(jnp.int32, (L, 1), 0) + c * L
    valid = row_id >= pad
    L2 = L // 2

    def qk_chunk(it):
        cs = slice(it * QK_CHUNK, (it + 1) * QK_CHUNK)
        qk_pre = proj(CB_Q + it * QK_CHUNK, QK_CHUNK)
        if pad:
            qk_pre = jnp.where(valid, qk_pre, 0.0)
        qkc[0:QK_HIST, :] = qkh[:, cs]
        qkc[QK_HIST:QK_HIST + L, :] = qk_pre
        acc = qk_pre * wqk[QK_CONV_K - 1:QK_CONV_K, cs] + bqk[:, cs]
        for j in range(QK_CONV_K - 1):
            acc = acc + qkc[QK_BASE + j:QK_BASE + j + L, :] * wqk[j:j + 1, cs]
        qkh[:, cs] = qkc[L:L + QK_HIST, :]
        acc = _silu(acc)
        if it * QK_CHUNK >= D_ML:
            acc = acc * (ML_HEAD_DIM ** -0.5)
        qks[:, cs] = _words(acc.astype(BF16))

    qk_chunk(0)
    qk_chunk(ML_HEADS)

    logf = jax.nn.log_sigmoid(g + fb[...])
    if pad:
        logf = jnp.where(valid, logf, 0.0)
    b_al = pltpu.roll(_cumsum_rows(logf, L), GATE_W - ML_HEADS, axis=1)
    m_prev = m_out[0]
    inter = b_al + m_prev
    d_src = g - b_al
    d_src_t = d_src.T
    tt = lax.broadcasted_iota(jnp.int32, (L, L), 0)
    ss = lax.broadcasted_iota(jnp.int32, (L, L), 1)
    mask = ss <= tt
    if pad:
        mask = mask & ((ss + c * L) >= pad)

    m_cols = []
    for h_i in range(ML_HEADS):
        hs = slice(h_i * ML_HEAD_DIM, (h_i + 1) * ML_HEAD_DIM)
        v = proj(CB_V + h_i * ML_HEAD_DIM, ML_HEAD_DIM)
        og = proj(CB_O + h_i * ML_HEAD_DIM, ML_HEAD_DIM)
        zm = proj(CB_ZM + h_i * ML_HEAD_DIM, ML_HEAD_DIM)
        if h_i + 1 < ML_HEADS:
            qk_chunk(h_i + 1)
            qk_chunk(ML_HEADS + h_i + 1)
        q = _halves(qks[:, hs])
        k = _halves(qks[:, D_ML + h_i * ML_HEAD_DIM:D_ML + (h_i + 1) * ML_HEAD_DIM])
        dmat = jnp.where(mask, b_al[:, h_i:h_i + 1] + d_src_t[h_i:h_i + 1, :], -jnp.inf)
        inter_h = inter[:, h_i:h_i + 1]
        m_t = jnp.maximum(inter_h, jnp.max(dmat, axis=-1, keepdims=True))
        w_intra = jnp.exp(dmat - m_t)
        w_inter = jnp.exp(inter_h - m_t)
        s_mat = lax.dot_general(q, k, (((1,), (1,)), ((), ())), preferred_element_type=F32) * w_intra
        c_old = c_out[0, h_i]
        n_old = n_out[0, h_i:h_i + 1, :]
        qc = lax.dot_general(q, c_old.astype(BF16), (((1,), (1,)), ((), ())), preferred_element_type=F32)
        num = jnp.dot(s_mat.astype(BF16), v.astype(BF16), preferred_element_type=F32) + w_inter * qc
        qn = jnp.sum(q.astype(F32) * n_old, axis=-1, keepdims=True)
        den = jnp.sum(s_mat, axis=-1, keepdims=True) + w_inter * qn
        hh = num * (1.0 / jnp.maximum(jnp.abs(den), jnp.exp(-m_t)))
        hh = hh * lax.rsqrt(jnp.mean(hh * hh, axis=-1, keepdims=True) + EPS) * mln[:, hs]
        hm = hh * _sigmoid(og) * _silu(zm)
        hm_scr[:, hs] = _words(hm.astype(BF16))

        m_new = m_t[L - 1:L, :]
        decay = jnp.exp(b_al[L - 1:L, h_i:h_i + 1] + m_prev[:, h_i:h_i + 1] - m_new)
        w_s = jnp.exp(d_src[:, h_i:h_i + 1] + (b_al[L - 1:L, h_i:h_i + 1] - m_new))
        if pad:
            w_s = jnp.where(valid, w_s, 0.0)
        vw = (w_s * v).astype(BF16)
        c_out[0, h_i] = decay * c_old + lax.dot_general(
            vw, k, (((0,), (0,)), ((), ())), preferred_element_type=F32)
        n_out[0, h_i:h_i + 1, :] = decay * n_old + jnp.sum(w_s * k.astype(F32), axis=0, keepdims=True)
        m_cols.append(m_new)
    qb_out[0] = qkh[QK_BASE:QK_HIST, :]

    lane = lax.broadcasted_iota(jnp.int32, (1, GATE_W), 1)
    m_row = jnp.zeros((1, GATE_W), F32)
    for h_i in range(ML_HEADS):
        m_row = jnp.where(lane == h_i, m_cols[h_i], m_row)
    m_out[0] = m_row

    br_m = _wdot(_halves(hm_scr[...]), wmo[...])
    ym = gcb_ref[0] + _sigmoid(proj(CB_GM, D_MODEL)) * br_m
    out = _wdot(ym.astype(BF16), wo[...])
    y_ref[0] = y_ref[0] + _rms(out, npost[...])


def _mstate_shapes(depth, S):
    return [
        (depth, S, QK_CONV_K - 1, 2 * D_ML),
        (depth, S, ML_HEADS, ML_HEAD_DIM, ML_HEAD_DIM),
        (depth, S, ML_HEADS, ML_HEAD_DIM),
        (depth, S, 1, GATE_W),
    ]


def _mlstm_branch(x, gcb, lw, layer, depth, L, *, meta=None, proj=None, state=None, prev_out=None,
                  pad=0, drop_first=False):
    fresh = proj is None
    first_block = meta is not None
    chained = prev_out is not None
    S, T = gcb.shape[0], gcb.shape[1]
    shapes = _mstate_shapes(depth, S)
    shift = lambda c: jnp.maximum(c - 1, 0)
    blk = lambda s, c: (s, c, 0)
    full2 = lambda s, c: (0, 0)
    const = dict(pipeline_mode=pl.Buffered(1))

    def state_spec(shape, **kw):
        nd = len(shape) - 2
        return pl.BlockSpec((None, 1) + shape[2:], lambda s, c: (layer, s) + (0,) * nd, **kw)

    in_specs = [pl.BlockSpec((1, L, D_MODEL), (lambda s, c: (s, shift(c), 0)) if first_block else blk)]
    args = [x]
    if first_block:
        in_specs.append(pl.BlockSpec(meta.shape, full2, **const))
        args.append(meta)
    in_specs.append(pl.BlockSpec((1, L, D_MODEL), blk))
    args.append(gcb)
    if fresh:
        ws = [lw["norm_pre"], lw["w_pb_words"], lw["b_pb"], lw["w_gate_words"], lw["b_gate"]]
    else:
        in_specs += [pl.BlockSpec((1, L, D_PB), blk), pl.BlockSpec((1, L, GATE_W), blk)]
        args += list(proj)
        in_specs += [state_spec(sh) for sh in shapes]
        args += list(state)
        ws = []
    ws += [lw["w_qkc"], lw["b_qkc"], lw["f_bias"], lw["ml_norm"], lw["w_ml_out_words"], lw["w_out_words"],
           lw["norm_post"]]
    in_specs += [pl.BlockSpec(w.shape, full2, **const) for w in ws]
    args += ws
    aliases = {}
    if chained:
        for k, a in enumerate(prev_out):
            aliases[len(args)] = 1 + k
            in_specs.append(pl.BlockSpec(memory_space=pl.ANY))
            args.append(a)

    t_out = T - L if drop_first else T
    out_shape = [jax.ShapeDtypeStruct((S, t_out, D_MODEL), F32)] + \
        [jax.ShapeDtypeStruct(sh, F32) for sh in shapes]
    out_specs = [pl.BlockSpec((1, L, D_MODEL), (lambda s, c: (s, shift(c), 0)) if drop_first else blk)] + \
        [state_spec(sh) for sh in shapes]
    scratch = [
        pltpu.VMEM((L + QK_HIST, QK_CHUNK), F32),
        pltpu.VMEM((QK_HIST, 2 * D_ML), F32),
        pltpu.VMEM((L // 2, 2 * D_ML), U32),
        pltpu.VMEM((L // 2, D_ML), U32),
    ]
    outs = pl.pallas_call(
        functools.partial(_mlstm_kernel, L=L, pad=pad, fresh=fresh, first_block=first_block,
                          chained=chained),
        grid=(S, T // L),
        in_specs=in_specs,
        out_specs=out_specs,
        out_shape=out_shape,
        scratch_shapes=scratch,
        input_output_aliases=aliases,
        compiler_params=_params(),
        name="mlstm_fresh" if fresh else "mlstm_carry",
    )(*args)
    return outs[0], outs[1:]


L_PROMPT = 256
TM_SAMPLE = 1024
TN_PROJ = 1024


def _layer_weights(l, norm_pre, norm_post, w_in, b_in, w_dw, b_dw, ln_g, ln_b, w_qk_conv, b_qk_conv,
                   f_bias, ml_norm, w_conv_out, w_ml_out, w_out):
    n_if = 2 * ML_HEADS
    lo_b = 3 * D_CONV
    lo_if = lo_b + 5 * D_ML
    lo_gc = lo_if + n_if
    lo_gm = lo_gc + D_MODEL
    w, b = w_in[l], b_in[l]
    w_pa = jnp.concatenate([w[:, :lo_b], w[:, lo_gc:lo_gm]], axis=1).astype(BF16)
    b_pa = jnp.concatenate([b[:lo_b], b[lo_gc:lo_gm]])[None, :]
    w_pb = jnp.concatenate([w[:, lo_b:lo_if], w[:, lo_gm:]], axis=1).astype(BF16)
    b_pb = jnp.concatenate([b[lo_b:lo_if], b[lo_gm:]])[None, :]
    w_gate = jnp.pad(w[:, lo_if:lo_gc], ((0, 0), (0, GATE_W - n_if))).astype(BF16)
    b_gate = jnp.pad(b[lo_if:lo_gc], (0, GATE_W - n_if))[None, :]
    fb = jnp.pad(f_bias[l], (ML_HEADS, GATE_W - n_if))[None, :]
    return dict(
        norm_pre=norm_pre[l][None, :], norm_post=norm_post[l][None, :],
        w_pa=w_pa, b_pa=b_pa, w_pb=w_pb, b_pb=b_pb, w_gate=w_gate, b_gate=b_gate,
        w_pa_words=_pack(w_pa), w_pb_words=_pack(w_pb), w_gate_words=_pack(w_gate),
        w_dw=w_dw[l], b_dw=b_dw[l][None, :], ln_g=ln_g[l][None, :], ln_b=ln_b[l][None, :],
        w_qkc=w_qk_conv[l], b_qkc=b_qk_conv[l][None, :], f_bias=fb, ml_norm=ml_norm[l][None, :],
        w_conv_out_words=_pack(w_conv_out[l].astype(BF16)),
        w_ml_out_words=_pack(w_ml_out[l].astype(BF16)),
        w_out_words=_pack(w_out[l].astype(BF16)))


def kernel(x_prompt, x_sample, state_conv, state_qk_conv, state_C, state_n, state_m, meta_tokens,
           norm_pre, norm_post, w_in, b_in, w_dw, b_dw, ln_g, ln_b, w_qk_conv, b_qk_conv, f_bias,
           ml_norm, w_conv_out, w_ml_out, w_out):
    bp, seq, _ = x_prompt.shape
    bs, dec_seq, _ = x_sample.shape
    depth = w_in.shape[0]
    L = L_PROMPT
    pad = L - N_META
    assert seq % L == 0 and pad % SUBLANES == 0 and dec_seq % (2 * SUBLANES) == 0

    meta = meta_tokens.astype(x_prompt.dtype)
    m_pad = jnp.pad(state_m, ((0, 0), (0, 0), (0, GATE_W - ML_HEADS)))[:, :, None, :]
    mstate_in = (state_qk_conv, state_C, state_n, m_pad)

    xp, xs = x_prompt, x_sample
    cb_p = cb_s = st_p = st_s = None
    for l in range(depth):
        lw = _layer_weights(l, norm_pre, norm_post, w_in, b_in, w_dw, b_dw, ln_g, ln_b, w_qk_conv,
                            b_qk_conv, f_bias, ml_norm, w_conv_out, w_ml_out, w_out)
        first = meta if l == 0 else None
        gcb, cb_p = _conv_branch(xp, lw, l, depth, L, meta=first, prev_out=cb_p, pad=pad)
        xp, st_p = _mlstm_branch(xp, gcb, lw, l, depth, L, meta=first, prev_out=st_p, pad=pad,
                                 drop_first=(l == depth - 1))

        xs2 = xs.reshape(bs * dec_seq, D_MODEL)
        (pa,) = _inproj(xs2, lw["norm_pre"], lw["w_pa"], lw["b_pa"], TM_SAMPLE, TN_PROJ)
        pb, pg = _inproj(xs2, lw["norm_pre"], lw["w_pb"], lw["b_pb"], TM_SAMPLE, TN_PROJ,
                         lw["w_gate"], lw["b_gate"])
        gcb, cb_s = _conv_branch(None, lw, l, depth, dec_seq, proj=pa.reshape(bs, dec_seq, D_PA),
                                 state=state_conv, prev_out=cb_s)
        xs, st_s = _mlstm_branch(xs, gcb, lw, l, depth, dec_seq,
                                 proj=(pb.reshape(bs, dec_seq, D_PB), pg.reshape(bs, dec_seq, GATE_W)),
                                 state=mstate_in, prev_out=st_s)

    fin = lambda cb, st: (cb,) + tuple(st[:3]) + (st[3][:, :, 0, :ML_HEADS],)
    return (xp, xs) + fin(cb_p, st_p) + fin(cb_s, st_s)
```

```python
import functools

import jax
import jax.numpy as jnp
from jax import lax
from jax.experimental import pallas as pl
from jax.experimental.pallas import tpu as pltpu

F32 = jnp.float32
BF16 = jnp.bfloat16
U32 = jnp.uint32

D_MODEL = 1024
D_CONV = D_MODEL
CONV_K = 31
D_ML = 2 * D_MODEL
ML_HEADS = 4
ML_HEAD_DIM = D_ML // ML_HEADS
QK_CONV_K = 4
N_META = 16
EPS = 1e-6

LANES = 128
SUBLANES = 8
GATE_W = LANES
N_IF = 2 * ML_HEADS
CONV_HIST = 32
CONV_BASE = CONV_HIST - (CONV_K - 1)
CONV_SPAN = (CONV_BASE + CONV_K - 2) // SUBLANES * SUBLANES
QK_HIST = SUBLANES
QK_BASE = QK_HIST - (QK_CONV_K - 1)
QK_CHUNK = ML_HEAD_DIM
CONV_ROWS = 128
TN = 1024
VMEM_LIMIT = 62 * 1024 * 1024

D_PA = 3 * D_CONV
D_PB = 5 * D_ML
LO_IF = D_PA + D_PB
LO_GC = LO_IF + N_IF
LO_GM = LO_GC + D_MODEL
CA_A, CA_GL, CA_ZC = 0, D_CONV, 2 * D_CONV
CB_Q, CB_K, CB_V, CB_O, CB_ZM = 0, D_ML, 2 * D_ML, 3 * D_ML, 4 * D_ML


def _sigmoid(x):
    return jax.nn.sigmoid(x)


def _silu(x):
    return x * jax.nn.sigmoid(x)


def _rms(x, g):
    return x * lax.rsqrt(jnp.mean(x * x, axis=-1, keepdims=True) + EPS) * g


def _words(x_bf16):
    return pltpu.bitcast(x_bf16, U32)


def _halves(x_u32):
    return pltpu.bitcast(x_u32, BF16)


def _wdot(x_bf16, w_words):
    return jnp.dot(x_bf16, _halves(w_words), preferred_element_type=F32)


def _params(n_grid):
    return pltpu.CompilerParams(dimension_semantics=("arbitrary",) * n_grid,
                                vmem_limit_bytes=VMEM_LIMIT)


def _pack_kernel(w_ref, o_ref):
    o_ref[...] = _words(w_ref[...].astype(BF16))


def _pack(w3d, layer, col_lo=0, n_col=None):
    _, k, n = w3d.shape
    n_col = n - col_lo if n_col is None else n_col
    tn = min(n_col, TN)
    assert col_lo % tn == 0 and n_col % tn == 0
    j0 = col_lo // tn
    return pl.pallas_call(
        _pack_kernel,
        grid=(n_col // tn,),
        in_specs=[pl.BlockSpec((None, k, tn), lambda j: (layer, 0, j + j0))],
        out_specs=pl.BlockSpec((k // 2, tn), lambda j: (0, j)),
        out_shape=jax.ShapeDtypeStruct((k // 2, n_col), U32),
        compiler_params=_params(1),
        name="pack_weight",
    )(w3d)


def _inproj_kernel(*refs, gated):
    if gated:
        x_ref, g_ref, w_ref, b_ref, wg_ref, bg_ref, o_ref, og_ref, h_ref = refs
    else:
        x_ref, g_ref, w_ref, b_ref, o_ref, h_ref = refs

    @pl.when(pl.program_id(1) == 0)
    def _():
        hb = _rms(x_ref[...], g_ref[...]).astype(BF16)
        h_ref[...] = _words(hb)
        if gated:
            og_ref[...] = _wdot(hb, wg_ref[...]) + bg_ref[...]

    o_ref[...] = _wdot(_halves(h_ref[...]), w_ref[...]) + b_ref[...]


def _inproj(x2d, g, w_words, b, tm, w_gate=None, b_gate=None):
    n_tok, n_col = x2d.shape[0], w_words.shape[1]
    kw = w_words.shape[0]
    gated = w_gate is not None
    in_specs = [
        pl.BlockSpec((tm, D_MODEL), lambda i, j: (i, 0)),
        pl.BlockSpec((1, D_MODEL), lambda i, j: (0, 0)),
        pl.BlockSpec((kw, TN), lambda i, j: (0, j)),
        pl.BlockSpec((1, TN), lambda i, j: (0, j)),
    ]
    out_specs = [pl.BlockSpec((tm, TN), lambda i, j: (i, j))]
    out_shape = [jax.ShapeDtypeStruct((n_tok, n_col), F32)]
    args = [x2d, g, w_words, b]
    if gated:
        in_specs += [pl.BlockSpec((kw, GATE_W), lambda i, j: (0, 0)),
                     pl.BlockSpec((1, GATE_W), lambda i, j: (0, 0))]
        out_specs.append(pl.BlockSpec((tm, GATE_W), lambda i, j: (i, 0)))
        out_shape.append(jax.ShapeDtypeStruct((n_tok, GATE_W), F32))
        args += [w_gate, b_gate]
    return pl.pallas_call(
        functools.partial(_inproj_kernel, gated=gated),
        grid=(n_tok // tm, n_col // TN),
        in_specs=in_specs, out_specs=out_specs, out_shape=out_shape,
        scratch_shapes=[pltpu.VMEM((tm // 2, D_MODEL), U32)],
        compiler_params=_params(2),
        name="inproj",
    )(*args)


def _cumsum_rows(x, n_rows):
    row = lax.broadcasted_iota(jnp.int32, x.shape, 0)
    k = 1
    while k < n_rows:
        x = x + jnp.where(row >= k, pltpu.roll(x, k, axis=0), 0.0)
        k *= 2
    return x


def _conv_kernel(*refs, L, pad, fresh, first_block, chained):
    refs = list(refs)
    x_ref = refs.pop(0) if fresh else None
    meta_ref = refs.pop(0) if first_block else None
    if fresh:
        npre, wpa, bpa, wgc, bgc = (refs.pop(0) for _ in range(5))
    else:
        pa_ref, pgc_ref, cb_in = (refs.pop(0) for _ in range(3))
    wdw, bdw, lng, lnb, wco = (refs.pop(0) for _ in range(5))
    if chained:
        refs.pop(0)
    gcb_ref, cb_out = refs.pop(0), refs.pop(0)
    ubuf, ush, cvo, zg = (refs.pop(0) for _ in range(4))
    h_scr = refs.pop(0) if fresh else None

    c = pl.program_id(1)

    @pl.when(c == 0)
    def _init():
        ubuf[0:CONV_HIST, :] = jnp.zeros((CONV_HIST, D_CONV), F32)
        if not fresh:
            ubuf[CONV_BASE:CONV_HIST, :] = cb_in[0]

    if fresh:
        x = x_ref[0]
        if first_block:
            x0 = jnp.concatenate([jnp.zeros((pad, D_MODEL), F32), meta_ref[...]], axis=0)
            x = jnp.where(c == 0, x0, x)
        h_scr[...] = _words(_rms(x, npre[...]).astype(BF16))

        def proj_a(off, width):
            return _wdot(_halves(h_scr[...]), wpa[:, off:off + width]) + bpa[:, off:off + width]

        def proj_gc(off, width):
            return _wdot(_halves(h_scr[...]), wgc[:, off:off + width]) + bgc[:, off:off + width]
    else:
        def proj_a(off, width):
            return pa_ref[0, :, off:off + width]

        def proj_gc(off, width):
            return pgc_ref[0, :, off:off + width]

    u = proj_a(CA_A, D_CONV) * _sigmoid(proj_a(CA_GL, D_CONV))
    if pad:
        row_id = lax.broadcasted_iota(jnp.int32, (L, 1), 0) + c * L
        u = jnp.where(row_id >= pad, u, 0.0)
    ubuf[CONV_HIST:CONV_HIST + L, :] = u

    rc = min(L, CONV_ROWS)
    n_strip = D_CONV // LANES
    zw = 2 * D_CONV // n_strip
    for it in range(n_strip):
        zo = it * zw
        zg[:, zo:zo + zw] = proj_a(CA_ZC + zo, zw) if zo < D_CONV else proj_gc(zo - D_CONV, zw)
        ls = slice(it * LANES, (it + 1) * LANES)
        for sh in range(1, SUBLANES):
            ush[sh - 1, :, :] = ubuf[sh:sh + L + CONV_SPAN, ls]
        for r0 in range(0, L, rc):
            acc = jnp.zeros((rc, LANES), F32) + bdw[:, ls]
            for j in range(CONV_K):
                a8, sh = divmod(CONV_BASE + j, SUBLANES)
                lo = a8 * SUBLANES + r0
                src = ubuf[lo:lo + rc, ls] if sh == 0 else ush[sh - 1, lo:lo + rc, :]
                acc = acc + src * wdw[j:j + 1, ls]
            cvo[r0:r0 + rc, ls] = acc
    cb_out[0] = ubuf[L + CONV_BASE:L + CONV_HIST, :]
    ubuf[0:CONV_HIST, :] = ubuf[L:L + CONV_HIST, :]

    cv = cvo[...]
    mu = jnp.mean(cv, axis=-1, keepdims=True)
    xc = cv - mu
    ln = xc * lax.rsqrt(jnp.mean(xc * xc, axis=-1, keepdims=True) + EPS) * lng[...] + lnb[...]
    ua = _silu(ln) * _silu(zg[:, 0:D_CONV])
    br_c = _wdot(ua.astype(BF16), wco[...])
    gcb_ref[0] = _sigmoid(zg[:, D_CONV:2 * D_CONV]) * br_c


def _conv_branch(x, lw, layer, depth, L, *, meta=None, proj=None, state=None, prev_out=None, pad=0):
    fresh = proj is None
    first_block = meta is not None
    chained = prev_out is not None
    src = x if fresh else proj[0]
    S = src.shape[0]
    T = src.shape[1] + (L if first_block else 0)
    shift = lambda c: jnp.maximum(c - 1, 0)
    blk = lambda s, c: (s, c, 0)
    full2 = lambda s, c: (0, 0)
    const = dict(pipeline_mode=pl.Buffered(1))
    st_shape = (depth, S, CONV_K - 1, D_CONV)
    st_spec = pl.BlockSpec((None, 1) + st_shape[2:], lambda s, c: (layer, s, 0, 0))

    in_specs, args = [], []
    if fresh:
        in_specs.append(pl.BlockSpec((1, L, D_MODEL), (lambda s, c: (s, shift(c), 0)) if first_block
                                     else blk))
        args.append(x)
        if first_block:
            in_specs.append(pl.BlockSpec(meta.shape, full2, **const))
            args.append(meta)
        ws = [lw["norm_pre"], lw["w_pa"], lw["b_pa"], lw["w_gc"], lw["b_gc"]]
    else:
        in_specs += [pl.BlockSpec((1, L, D_PA), blk), pl.BlockSpec((1, L, D_MODEL), blk), st_spec]
        args += [proj[0], proj[1], state]
        ws = []
    ws += [lw["w_dw"], lw["b_dw"], lw["ln_g"], lw["ln_b"], lw["w_conv_out"]]
    in_specs += [pl.BlockSpec(w.shape, full2, **const) for w in ws]
    args += ws
    aliases = {}
    if chained:
        aliases[len(args)] = 1
        in_specs.append(pl.BlockSpec(memory_space=pl.ANY))
        args.append(prev_out)

    scratch = [
        pltpu.VMEM((L + CONV_HIST, D_CONV), F32),
        pltpu.VMEM((SUBLANES - 1, L + CONV_SPAN, LANES), F32),
        pltpu.VMEM((L, D_CONV), F32),
        pltpu.VMEM((L, 2 * D_CONV), F32),
    ]
    if fresh:
        scratch.append(pltpu.VMEM((L // 2, D_MODEL), U32))
    return pl.pallas_call(
        functools.partial(_conv_kernel, L=L, pad=pad, fresh=fresh, first_block=first_block,
                          chained=chained),
        grid=(S, T // L),
        in_specs=in_specs,
        out_specs=[pl.BlockSpec((1, L, D_MODEL), blk), st_spec],
        out_shape=[jax.ShapeDtypeStruct((S, T, D_MODEL), F32), jax.ShapeDtypeStruct(st_shape, F32)],
        scratch_shapes=scratch,
        input_output_aliases=aliases,
        compiler_params=_params(2),
        name="conv_fresh" if fresh else "conv_carry",
    )(*args)


N_MSTATE = 4


def _mlstm_kernel(*refs, L, pad, fresh, first_block, chained):
    refs = list(refs)
    x_ref = refs.pop(0)
    meta_ref = refs.pop(0) if first_block else None
    gcb_ref = refs.pop(0)
    if fresh:
        npre, wpb, bpb, wgm, bgm, wgate, bgate = (refs.pop(0) for _ in range(7))
    else:
        pb_ref, pgm_ref, pg_ref = (refs.pop(0) for _ in range(3))
        qb_in, c_in, n_in, m_in = (refs.pop(0) for _ in range(N_MSTATE))
    wqk, bqk, fb, mln, wmo, wo, npost = (refs.pop(0) for _ in range(7))
    if chained:
        del refs[:N_MSTATE]
    y_ref, qb_out, c_out, n_out, m_out = (refs.pop(0) for _ in range(1 + N_MSTATE))
    qkc, qkh, qks, hm_scr, cw = (refs.pop(0) for _ in range(5))
    h_scr = refs.pop(0) if fresh else None

    c = pl.program_id(1)

    @pl.when(c == 0)
    def _init():
        qkh[...] = jnp.zeros(qkh.shape, F32)
        if fresh:
            c_out[...] = jnp.zeros(c_out.shape, F32)
            n_out[...] = jnp.zeros(n_out.shape, F32)
            m_out[...] = jnp.zeros(m_out.shape, F32)
            cw[...] = jnp.zeros(cw.shape, U32)
        else:
            qkh[QK_BASE:QK_HIST, :] = qb_in[0]
            c_out[...] = c_in[...]
            n_out[...] = n_in[...]
            m_out[...] = m_in[...]
            for h_i in range(ML_HEADS):
                cw[h_i] = _words(c_in[0, h_i].astype(BF16))

    y_ref[0] = x_ref[0]
    if first_block:
        @pl.when(c == 0)
        def _first():
            y_ref[0, 0:pad, :] = jnp.zeros((pad, D_MODEL), F32)
            y_ref[0, pad:L, :] = meta_ref[...]

    if fresh:
        h_scr[...] = _words(_rms(y_ref[0], npre[...]).astype(BF16))

        def proj(off, width):
            return _wdot(_halves(h_scr[...]), wpb[:, off:off + width]) + bpb[:, off:off + width]

        g = _wdot(_halves(h_scr[...]), wgate[...]) + bgate[...]
    else:
        def proj(off, width):
            return pb_ref[0, :, off:off + width]

        g = pg_ref[0]

    row_id = lax.broadcasted_iota(jnp.int32, (L, 1), 0) + c * L
    valid = row_id >= pad

    def qk_chunk(it):
        cs = slice(it * QK_CHUNK, (it + 1) * QK_CHUNK)
        qk_pre = proj(CB_Q + it * QK_CHUNK, QK_CHUNK)
        if pad:
            qk_pre = jnp.where(valid, qk_pre, 0.0)
        qkc[0:QK_HIST, :] = qkh[:, cs]
        qkc[QK_HIST:QK_HIST + L, :] = qk_pre
        acc = qk_pre * wqk[QK_CONV_K - 1:QK_CONV_K, cs] + bqk[:, cs]
        for j in range(QK_CONV_K - 1):
            acc = acc + qkc[QK_BASE + j:QK_BASE + j + L, :] * wqk[j:j + 1, cs]
        qkh[:, cs] = qkc[L:L + QK_HIST, :]
        acc = _silu(acc)
        if it * QK_CHUNK >= D_ML:
            acc = acc * (ML_HEAD_DIM ** -0.5)
        qks[:, cs] = _words(acc.astype(BF16))

    qk_chunk(0)
    qk_chunk(ML_HEADS)

    logf = jax.nn.log_sigmoid(g + fb[...])
    if pad:
        logf = jnp.where(valid, logf, 0.0)
    b_al = pltpu.roll(_cumsum_rows(logf, L), GATE_W - ML_HEADS, axis=1)
    m_prev = m_out[0]
    inter = b_al + m_prev
    d_src = g - b_al
    d_src_t = d_src.T
    tt = lax.broadcasted_iota(jnp.int32, (L, L), 0)
    ss = lax.broadcasted_iota(jnp.int32, (L, L), 1)
    mask = ss <= tt
    if pad:
        mask = mask & ((ss + c * L) >= pad)

    m_cols = []
    for h_i in range(ML_HEADS):
        hs = slice(h_i * ML_HEAD_DIM, (h_i + 1) * ML_HEAD_DIM)
        v = proj(CB_V + h_i * ML_HEAD_DIM, ML_HEAD_DIM)
        og = proj(CB_O + h_i * ML_HEAD_DIM, ML_HEAD_DIM)
        zm = proj(CB_ZM + h_i * ML_HEAD_DIM, ML_HEAD_DIM)
        if h_i + 1 < ML_HEADS:
            qk_chunk(h_i + 1)
            qk_chunk(ML_HEADS + h_i + 1)
        q = _halves(qks[:, hs])
        k = _halves(qks[:, D_ML + h_i * ML_HEAD_DIM:D_ML + (h_i + 1) * ML_HEAD_DIM])
        dmat = jnp.where(mask, b_al[:, h_i:h_i + 1] + d_src_t[h_i:h_i + 1, :], -jnp.inf)
        inter_h = inter[:, h_i:h_i + 1]
        m_t = jnp.maximum(inter_h, jnp.max(dmat, axis=-1, keepdims=True))
        w_intra = jnp.exp(dmat - m_t)
        w_inter = jnp.exp(inter_h - m_t)
        s_mat = lax.dot_general(q, k, (((1,), (1,)), ((), ())), preferred_element_type=F32) * w_intra
        n_old = n_out[0, h_i:h_i + 1, :]
        qc = lax.dot_general(q, _halves(cw[h_i]), (((1,), (1,)), ((), ())), preferred_element_type=F32)
        num = jnp.dot(s_mat.astype(BF16), v.astype(BF16), preferred_element_type=F32) + w_inter * qc
        qn = jnp.sum(q.astype(F32) * n_old, axis=-1, keepdims=True)
        den = jnp.sum(s_mat, axis=-1, keepdims=True) + w_inter * qn
        hh = num * (1.0 / jnp.maximum(jnp.abs(den), jnp.exp(-m_t)))
        hh = hh * lax.rsqrt(jnp.mean(hh * hh, axis=-1, keepdims=True) + EPS) * mln[:, hs]
        hm = hh * _sigmoid(og) * _silu(zm)
        hm_scr[:, hs] = _words(hm.astype(BF16))

        m_new = m_t[L - 1:L, :]
        decay = jnp.exp(b_al[L - 1:L, h_i:h_i + 1] + m_prev[:, h_i:h_i + 1] - m_new)
        w_s = jnp.exp(d_src[:, h_i:h_i + 1] + (b_al[L - 1:L, h_i:h_i + 1] - m_new))
        if pad:
            w_s = jnp.where(valid, w_s, 0.0)
        vw = (w_s * v).astype(BF16)
        c_new = decay * c_out[0, h_i] + lax.dot_general(
            vw, k, (((0,), (0,)), ((), ())), preferred_element_type=F32)
        c_out[0, h_i] = c_new
        cw[h_i] = _words(c_new.astype(BF16))
        n_out[0, h_i:h_i + 1, :] = decay * n_old + jnp.sum(w_s * k.astype(F32), axis=0, keepdims=True)
        m_cols.append(m_new)
    qb_out[0] = qkh[QK_BASE:QK_HIST, :]

    lane = lax.broadcasted_iota(jnp.int32, (1, GATE_W), 1)
    m_row = jnp.zeros((1, GATE_W), F32)
    for h_i in range(ML_HEADS):
        m_row = jnp.where(lane == h_i, m_cols[h_i], m_row)
    m_out[0] = m_row

    if fresh:
        gm = _wdot(_halves(h_scr[...]), wgm[...]) + bgm[...]
    else:
        gm = pgm_ref[0]
    br_m = _wdot(_halves(hm_scr[...]), wmo[...])
    ym = gcb_ref[0] + _sigmoid(gm) * br_m
    out = _wdot(ym.astype(BF16), wo[...])
    y_ref[0] = y_ref[0] + _rms(out, npost[...])


def _mstate_shapes(depth, S):
    return [
        (depth, S, QK_CONV_K - 1, 2 * D_ML),
        (depth, S, ML_HEADS, ML_HEAD_DIM, ML_HEAD_DIM),
        (depth, S, ML_HEADS, ML_HEAD_DIM),
        (depth, S, 1, GATE_W),
    ]


def _mlstm_branch(x, gcb, lw, layer, depth, L, *, meta=None, proj=None, state=None, prev_out=None,
                  pad=0, drop_first=False):
    fresh = proj is None
    first_block = meta is not None
    chained = prev_out is not None
    S, T = gcb.shape[0], gcb.shape[1]
    shapes = _mstate_shapes(depth, S)
    shift = lambda c: jnp.maximum(c - 1, 0)
    blk = lambda s, c: (s, c, 0)
    full2 = lambda s, c: (0, 0)
    const = dict(pipeline_mode=pl.Buffered(1))

    def state_spec(shape, **kw):
        nd = len(shape) - 2
        return pl.BlockSpec((None, 1) + shape[2:], lambda s, c: (layer, s) + (0,) * nd, **kw)

    in_specs = [pl.BlockSpec((1, L, D_MODEL), (lambda s, c: (s, shift(c), 0)) if first_block else blk)]
    args = [x]
    if first_block:
        in_specs.append(pl.BlockSpec(meta.shape, full2, **const))
        args.append(meta)
    in_specs.append(pl.BlockSpec((1, L, D_MODEL), blk))
    args.append(gcb)
    if fresh:
        ws = [lw["norm_pre"], lw["w_pb"], lw["b_pb"], lw["w_gm"], lw["b_gm"], lw["w_gate"], lw["b_gate"]]
    else:
        in_specs += [pl.BlockSpec((1, L, D_PB), blk), pl.BlockSpec((1, L, D_MODEL), blk),
                     pl.BlockSpec((1, L, GATE_W), blk)]
        args += list(proj)
        in_specs += [state_spec(sh) for sh in shapes]
        args += list(state)
        ws = []
    ws += [lw["w_qkc"], lw["b_qkc"], lw["f_bias"], lw["ml_norm"], lw["w_ml_out"], lw["w_out"],
           lw["norm_post"]]
    in_specs += [pl.BlockSpec(w.shape, full2, **const) for w in ws]
    args += ws
    aliases = {}
    if chained:
        for k, a in enumerate(prev_out):
            aliases[len(args)] = 1 + k
            in_specs.append(pl.BlockSpec(memory_space=pl.ANY))
            args.append(a)

    t_out = T - L if drop_first else T
    out_shape = [jax.ShapeDtypeStruct((S, t_out, D_MODEL), F32)] + \
        [jax.ShapeDtypeStruct(sh, F32) for sh in shapes]
    out_specs = [pl.BlockSpec((1, L, D_MODEL), (lambda s, c: (s, shift(c), 0)) if drop_first else blk)] + \
        [state_spec(sh) for sh in shapes]
    scratch = [
        pltpu.VMEM((L + QK_HIST, QK_CHUNK), F32),
        pltpu.VMEM((QK_HIST, 2 * D_ML), F32),
        pltpu.VMEM((L // 2, 2 * D_ML), U32),
        pltpu.VMEM((L // 2, D_ML), U32),
        pltpu.VMEM((ML_HEADS, ML_HEAD_DIM // 2, ML_HEAD_DIM), U32),
    ]
    if fresh:
        scratch.append(pltpu.VMEM((L // 2, D_MODEL), U32))
    outs = pl.pallas_call(
        functools.partial(_mlstm_kernel, L=L, pad=pad, fresh=fresh, first_block=first_block,
                          chained=chained),
        grid=(S, T // L),
        in_specs=in_specs,
        out_specs=out_specs,
        out_shape=out_shape,
        scratch_shapes=scratch,
        input_output_aliases=aliases,
        compiler_params=_params(2),
        name="mlstm_fresh" if fresh else "mlstm_carry",
    )(*args)
    return outs[0], outs[1:]


L_PROMPT = 256
TM_SAMPLE = 1024


def _layer_weights(l, norm_pre, norm_post, w_in, b_in, w_dw, b_dw, ln_g, ln_b, w_qk_conv, b_qk_conv,
                   f_bias, ml_norm, w_conv_out, w_ml_out, w_out):
    b = b_in[l]
    tail = w_in[l, :, LO_IF:]
    w_gate = jnp.pad(tail[:, :N_IF], ((0, 0), (0, GATE_W - N_IF)))[None]
    w_gc = tail[:, N_IF:N_IF + D_MODEL][None]
    w_gm = tail[:, N_IF + D_MODEL:][None]
    row = lambda a: a[None, :]
    return dict(
        norm_pre=row(norm_pre[l]), norm_post=row(norm_post[l]),
        w_pa=_pack(w_in, l, 0, D_PA), b_pa=row(b[:D_PA]),
        w_pb=_pack(w_in, l, D_PA, D_PB), b_pb=row(b[D_PA:LO_IF]),
        w_gate=_pack(w_gate, 0), b_gate=row(jnp.pad(b[LO_IF:LO_GC], (0, GATE_W - N_IF))),
        w_gc=_pack(w_gc, 0), b_gc=row(b[LO_GC:LO_GM]),
        w_gm=_pack(w_gm, 0), b_gm=row(b[LO_GM:]),
        w_dw=w_dw[l], b_dw=row(b_dw[l]), ln_g=row(ln_g[l]), ln_b=row(ln_b[l]),
        w_qkc=w_qk_conv[l], b_qkc=row(b_qk_conv[l]),
        f_bias=row(jnp.pad(f_bias[l], (ML_HEADS, GATE_W - N_IF))), ml_norm=row(ml_norm[l]),
        w_conv_out=_pack(w_conv_out, l), w_ml_out=_pack(w_ml_out, l), w_out=_pack(w_out, l))


def kernel(x_prompt, x_sample, state_conv, state_qk_conv, state_C, state_n, state_m, meta_tokens,
           norm_pre, norm_post, w_in, b_in, w_dw, b_dw, ln_g, ln_b, w_qk_conv, b_qk_conv, f_bias,
           ml_norm, w_conv_out, w_ml_out, w_out):
    bp, seq, _ = x_prompt.shape
    bs, dec_seq, _ = x_sample.shape
    depth = w_in.shape[0]
    L = L_PROMPT
    pad = L - N_META
    assert seq % L == 0 and pad % SUBLANES == 0 and dec_seq % (2 * SUBLANES) == 0

    meta = meta_tokens.astype(x_prompt.dtype)
    m_pad = jnp.pad(state_m, ((0, 0), (0, 0), (0, GATE_W - ML_HEADS)))[:, :, None, :]
    mstate_in = (state_qk_conv, state_C, state_n, m_pad)

    xp, xs = x_prompt, x_sample
    cb_p = cb_s = st_p = st_s = None
    for l in range(depth):
        lw = _layer_weights(l, norm_pre, norm_post, w_in, b_in, w_dw, b_dw, ln_g, ln_b, w_qk_conv,
                            b_qk_conv, f_bias, ml_norm, w_conv_out, w_ml_out, w_out)
        first = meta if l == 0 else None
        gcb, cb_p = _conv_branch(xp, lw, l, depth, L, meta=first, prev_out=cb_p, pad=pad)
        xp, st_p = _mlstm_branch(xp, gcb, lw, l, depth, L, meta=first, prev_out=st_p, pad=pad,
                                 drop_first=(l == depth - 1))

        xs2 = xs.reshape(bs * dec_seq, D_MODEL)
        tok = lambda a: a.reshape(bs, dec_seq, a.shape[-1])
        (pa,) = _inproj(xs2, lw["norm_pre"], lw["w_pa"], lw["b_pa"], TM_SAMPLE)
        (pgc,) = _inproj(xs2, lw["norm_pre"], lw["w_gc"], lw["b_gc"], TM_SAMPLE)
        pb, pg = _inproj(xs2, lw["norm_pre"], lw["w_pb"], lw["b_pb"], TM_SAMPLE, lw["w_gate"], lw["b_gate"])
        (pgm,) = _inproj(xs2, lw["norm_pre"], lw["w_gm"], lw["b_gm"], TM_SAMPLE)
        gcb, cb_s = _conv_branch(None, lw, l, depth, dec_seq, proj=(tok(pa), tok(pgc)),
                                 state=state_conv, prev_out=cb_s)
        xs, st_s = _mlstm_branch(xs, gcb, lw, l, depth, dec_seq, proj=(tok(pb), tok(pgm), tok(pg)),
                                 state=mstate_in, prev_out=st_s)

    fin = lambda cb, st: (cb,) + tuple(st[:3]) + (st[3][:, :, 0, :ML_HEADS],)
    return (xp, xs) + fin(cb_p, st_p) + fin(cb_s, st_s)
```

```python
import functools

import jax
import jax.numpy as jnp
from jax import lax
from jax.experimental import pallas as pl
from jax.experimental.pallas import tpu as pltpu

F32 = jnp.float32
BF16 = jnp.bfloat16
U32 = jnp.uint32

D_MODEL = 1024
D_CONV = D_MODEL
CONV_K = 31
D_ML = 2 * D_MODEL
ML_HEADS = 4
ML_HEAD_DIM = D_ML // ML_HEADS
QK_CONV_K = 4
N_META = 16
EPS = 1e-6

LANES = 128
SUBLANES = 8
GATE_W = LANES
N_IF = 2 * ML_HEADS
CONV_HIST = 32
CONV_BASE = CONV_HIST - (CONV_K - 1)
CONV_SPAN = (CONV_BASE + CONV_K - 2) // SUBLANES * SUBLANES
QK_HIST = SUBLANES
QK_BASE = QK_HIST - (QK_CONV_K - 1)
QK_CHUNK = ML_HEAD_DIM
CONV_ROWS = 128
GLU_CHUNK = 2 * LANES
TN = 1024
VMEM_LIMIT = 62 * 1024 * 1024

D_PA = 3 * D_CONV
D_PB = 5 * D_ML
LO_IF = D_PA + D_PB
LO_GC = LO_IF + N_IF
LO_GM = LO_GC + D_MODEL
CA_A, CA_GL, CA_ZC = 0, D_CONV, 2 * D_CONV
CB_Q, CB_K, CB_V, CB_O, CB_ZM = 0, D_ML, 2 * D_ML, 3 * D_ML, 4 * D_ML


def _sigmoid(x):
    return jax.nn.sigmoid(x)


def _silu(x):
    return x * jax.nn.sigmoid(x)


def _rms(x, g):
    return x * lax.rsqrt(jnp.mean(x * x, axis=-1, keepdims=True) + EPS) * g


def _words(x_bf16):
    return pltpu.bitcast(x_bf16, U32)


def _halves(x_u32):
    return pltpu.bitcast(x_u32, BF16)


def _wdot(x_bf16, w_words):
    return jnp.dot(x_bf16, _halves(w_words), preferred_element_type=F32)


def _params(n_grid):
    return pltpu.CompilerParams(dimension_semantics=("arbitrary",) * n_grid,
                                vmem_limit_bytes=VMEM_LIMIT)


def _pack_kernel(w_ref, o_ref):
    o_ref[...] = _words(w_ref[...].astype(BF16))


def _pack(w3d, layer, col_lo=0, n_col=None):
    _, k, n = w3d.shape
    n_col = n - col_lo if n_col is None else n_col
    tn = min(n_col, TN)
    assert col_lo % tn == 0 and n_col % tn == 0
    j0 = col_lo // tn
    return pl.pallas_call(
        _pack_kernel,
        grid=(n_col // tn,),
        in_specs=[pl.BlockSpec((None, k, tn), lambda j: (layer, 0, j + j0))],
        out_specs=pl.BlockSpec((k // 2, tn), lambda j: (0, j)),
        out_shape=jax.ShapeDtypeStruct((k // 2, n_col), U32),
        compiler_params=_params(1),
        name="pack_weight",
    )(w3d)


def _inproj_kernel(*refs, gated):
    if gated:
        x_ref, g_ref, w_ref, b_ref, wg_ref, bg_ref, o_ref, og_ref, h_ref = refs
    else:
        x_ref, g_ref, w_ref, b_ref, o_ref, h_ref = refs

    @pl.when(pl.program_id(1) == 0)
    def _():
        hb = _rms(x_ref[...], g_ref[...]).astype(BF16)
        h_ref[...] = _words(hb)
        if gated:
            og_ref[...] = _wdot(hb, wg_ref[...]) + bg_ref[...]

    o_ref[...] = _wdot(_halves(h_ref[...]), w_ref[...]) + b_ref[...]


def _inproj(x2d, g, w_words, b, tm, w_gate=None, b_gate=None):
    n_tok, n_col = x2d.shape[0], w_words.shape[1]
    kw = w_words.shape[0]
    gated = w_gate is not None
    in_specs = [
        pl.BlockSpec((tm, D_MODEL), lambda i, j: (i, 0)),
        pl.BlockSpec((1, D_MODEL), lambda i, j: (0, 0)),
        pl.BlockSpec((kw, TN), lambda i, j: (0, j)),
        pl.BlockSpec((1, TN), lambda i, j: (0, j)),
    ]
    out_specs = [pl.BlockSpec((tm, TN), lambda i, j: (i, j))]
    out_shape = [jax.ShapeDtypeStruct((n_tok, n_col), F32)]
    args = [x2d, g, w_words, b]
    if gated:
        in_specs += [pl.BlockSpec((kw, GATE_W), lambda i, j: (0, 0)),
                     pl.BlockSpec((1, GATE_W), lambda i, j: (0, 0))]
        out_specs.append(pl.BlockSpec((tm, GATE_W), lambda i, j: (i, 0)))
        out_shape.append(jax.ShapeDtypeStruct((n_tok, GATE_W), F32))
        args += [w_gate, b_gate]
    return pl.pallas_call(
        functools.partial(_inproj_kernel, gated=gated),
        grid=(n_tok // tm, n_col // TN),
        in_specs=in_specs, out_specs=out_specs, out_shape=out_shape,
        scratch_shapes=[pltpu.VMEM((tm // 2, D_MODEL), U32)],
        compiler_params=_params(2),
        name="inproj",
    )(*args)


def _cumsum_rows(x, n_rows):
    row = lax.broadcasted_iota(jnp.int32, x.shape, 0)
    k = 1
    while k < n_rows:
        x = x + jnp.where(row >= k, pltpu.roll(x, k, axis=0), 0.0)
        k *= 2
    return x


def _conv_kernel(*refs, L, pad, fresh, first_block, chained):
    refs = list(refs)
    x_ref = refs.pop(0) if fresh else None
    meta_ref = refs.pop(0) if first_block else None
    if fresh:
        npre, wpa, bpa, wgc, bgc = (refs.pop(0) for _ in range(5))
    else:
        pa_ref, pgc_ref, cb_in = (refs.pop(0) for _ in range(3))
    wdw, bdw, lng, lnb, wco = (refs.pop(0) for _ in range(5))
    if chained:
        refs.pop(0)
    gcb_ref, cb_out = refs.pop(0), refs.pop(0)
    ubuf, ush, cvo = (refs.pop(0) for _ in range(3))

    c = pl.program_id(1)

    @pl.when(c == 0)
    def _init():
        ubuf[0:CONV_HIST, :] = jnp.zeros((CONV_HIST, D_CONV), F32)
        if not fresh:
            ubuf[CONV_BASE:CONV_HIST, :] = cb_in[0]

    if fresh:
        x = x_ref[0]
        if first_block:
            x0 = jnp.concatenate([jnp.zeros((pad, D_MODEL), F32), meta_ref[...]], axis=0)
            x = jnp.where(c == 0, x0, x)
        h = _rms(x, npre[...]).astype(BF16)

        def proj_a(off, width):
            return _wdot(h, wpa[:, off:off + width]) + bpa[:, off:off + width]

        def proj_gc(off, width):
            return _wdot(h, wgc[:, off:off + width]) + bgc[:, off:off + width]
    else:
        def proj_a(off, width):
            return pa_ref[0, :, off:off + width]

        def proj_gc(off, width):
            return pgc_ref[0, :, off:off + width]

    if pad:
        row_id = lax.broadcasted_iota(jnp.int32, (L, 1), 0) + c * L
        valid = row_id >= pad

    rc = min(L, CONV_ROWS)
    n_chunk = D_CONV // GLU_CHUNK
    for ck in range(n_chunk):
        co = ck * GLU_CHUNK
        u = proj_a(CA_A + co, GLU_CHUNK) * _sigmoid(proj_a(CA_GL + co, GLU_CHUNK))
        if pad:
            u = jnp.where(valid, u, 0.0)
        ubuf[CONV_HIST:CONV_HIST + L, co:co + GLU_CHUNK] = u
        if ck == n_chunk - 2:
            zc = proj_a(CA_ZC, D_CONV)
        if ck == n_chunk - 1:
            gc = proj_gc(0, D_MODEL)
        for it in range(co // LANES, (co + GLU_CHUNK) // LANES):
            ls = slice(it * LANES, (it + 1) * LANES)
            sb = (it % 2) * (SUBLANES - 1)
            for sh in range(1, SUBLANES):
                ush[sb + sh - 1, :, :] = ubuf[sh:sh + L + CONV_SPAN, ls]
            for r0 in range(0, L, rc):
                acc = jnp.zeros((rc, LANES), F32) + bdw[:, ls]
                for j in range(CONV_K):
                    a8, sh = divmod(CONV_BASE + j, SUBLANES)
                    lo = a8 * SUBLANES + r0
                    src = ubuf[lo:lo + rc, ls] if sh == 0 else ush[sb + sh - 1, lo:lo + rc, :]
                    acc = acc + src * wdw[j:j + 1, ls]
                cvo[r0:r0 + rc, ls] = acc
    cb_out[0] = ubuf[L + CONV_BASE:L + CONV_HIST, :]
    ubuf[0:CONV_HIST, :] = ubuf[L:L + CONV_HIST, :]

    cv = cvo[...]
    mu = jnp.mean(cv, axis=-1, keepdims=True)
    xc = cv - mu
    ln = xc * lax.rsqrt(jnp.mean(xc * xc, axis=-1, keepdims=True) + EPS) * lng[...] + lnb[...]
    ua = _silu(ln) * _silu(zc)
    br_c = _wdot(ua.astype(BF16), wco[...])
    gcb_ref[0] = _sigmoid(gc) * br_c


def _conv_branch(x, lw, layer, depth, L, *, meta=None, proj=None, state=None, prev_out=None, pad=0):
    fresh = proj is None
    first_block = meta is not None
    chained = prev_out is not None
    src = x if fresh else proj[0]
    S = src.shape[0]
    T = src.shape[1] + (L if first_block else 0)
    shift = lambda c: jnp.maximum(c - 1, 0)
    blk = lambda s, c: (s, c, 0)
    full2 = lambda s, c: (0, 0)
    const = dict(pipeline_mode=pl.Buffered(1))
    st_shape = (depth, S, CONV_K - 1, D_CONV)
    st_spec = pl.BlockSpec((None, 1) + st_shape[2:], lambda s, c: (layer, s, 0, 0))

    in_specs, args = [], []
    if fresh:
        in_specs.append(pl.BlockSpec((1, L, D_MODEL), (lambda s, c: (s, shift(c), 0)) if first_block
                                     else blk))
        args.append(x)
        if first_block:
            in_specs.append(pl.BlockSpec(meta.shape, full2, **const))
            args.append(meta)
        ws = [lw["norm_pre"], lw["w_pa"], lw["b_pa"], lw["w_gc"], lw["b_gc"]]
    else:
        in_specs += [pl.BlockSpec((1, L, D_PA), blk), pl.BlockSpec((1, L, D_MODEL), blk), st_spec]
        args += [proj[0], proj[1], state]
        ws = []
    ws += [lw["w_dw"], lw["b_dw"], lw["ln_g"], lw["ln_b"], lw["w_conv_out"]]
    in_specs += [pl.BlockSpec(w.shape, full2, **const) for w in ws]
    args += ws
    aliases = {}
    if chained:
        aliases[len(args)] = 1
        in_specs.append(pl.BlockSpec(memory_space=pl.ANY))
        args.append(prev_out)

    scratch = [
        pltpu.VMEM((L + CONV_HIST, D_CONV), F32),
        pltpu.VMEM((2 * (SUBLANES - 1), L + CONV_SPAN, LANES), F32),
        pltpu.VMEM((L, D_CONV), F32),
    ]
    return pl.pallas_call(
        functools.partial(_conv_kernel, L=L, pad=pad, fresh=fresh, first_block=first_block,
                          chained=chained),
        grid=(S, T // L),
        in_specs=in_specs,
        out_specs=[pl.BlockSpec((1, L, D_MODEL), blk), st_spec],
        out_shape=[jax.ShapeDtypeStruct((S, T, D_MODEL), F32), jax.ShapeDtypeStruct(st_shape, F32)],
        scratch_shapes=scratch,
        input_output_aliases=aliases,
        compiler_params=_params(2),
        name="conv_fresh" if fresh else "conv_carry",
    )(*args)


N_MSTATE = 4


def _mlstm_kernel(*refs, L, pad, fresh, first_block, chained):
    refs = list(refs)
    x_ref = refs.pop(0)
    meta_ref = refs.pop(0) if first_block else None
    gcb_ref = refs.pop(0)
    if fresh:
        npre, wpb, bpb, wgm, bgm, wgate, bgate = (refs.pop(0) for _ in range(7))
    else:
        pb_ref, pgm_ref, pg_ref = (refs.pop(0) for _ in range(3))
        qb_in, c_in, n_in, m_in = (refs.pop(0) for _ in range(N_MSTATE))
    wqk, bqk, fb, mln, wmo, wo, npost = (refs.pop(0) for _ in range(7))
    if chained:
        del refs[:N_MSTATE]
    y_ref, qb_out, c_out, n_out, m_out = (refs.pop(0) for _ in range(1 + N_MSTATE))
    qkc, qkh, qks, hm_scr, cw = (refs.pop(0) for _ in range(5))
    h_scr = refs.pop(0) if fresh else None

    c = pl.program_id(1)

    @pl.when(c == 0)
    def _init():
        qkh[...] = jnp.zeros(qkh.shape, F32)
        if fresh:
            c_out[...] = jnp.zeros(c_out.shape, F32)
            n_out[...] = jnp.zeros(n_out.shape, F32)
            m_out[...] = jnp.zeros(m_out.shape, F32)
            cw[...] = jnp.zeros(cw.shape, U32)
        else:
            qkh[QK_BASE:QK_HIST, :] = qb_in[0]
            c_out[...] = c_in[...]
            n_out[...] = n_in[...]
            m_out[...] = m_in[...]
            for h_i in range(ML_HEADS):
                cw[h_i] = _words(c_in[0, h_i].astype(BF16))

    y_ref[0] = x_ref[0]
    if first_block:
        @pl.when(c == 0)
        def _first():
            y_ref[0, 0:pad, :] = jnp.zeros((pad, D_MODEL), F32)
            y_ref[0, pad:L, :] = meta_ref[...]

    if fresh:
        h_scr[...] = _words(_rms(y_ref[0], npre[...]).astype(BF16))

        def proj(off, width):
            return _wdot(_halves(h_scr[...]), wpb[:, off:off + width]) + bpb[:, off:off + width]

        g = _wdot(_halves(h_scr[...]), wgate[...]) + bgate[...]
    else:
        def proj(off, width):
            return pb_ref[0, :, off:off + width]

        g = pg_ref[0]

    row_id = lax.broadcasted_iota(jnp.int32, (L, 1), 0) + c * L
    valid = row_id >= pad

    def qk_chunk(it):
        cs = slice(it * QK_CHUNK, (it + 1) * QK_CHUNK)
        qk_pre = proj(CB_Q + it * QK_CHUNK, QK_CHUNK)
        if pad:
            qk_pre = jnp.where(valid, qk_pre, 0.0)
        qkc[0:QK_HIST, :] = qkh[:, cs]
        qkc[QK_HIST:QK_HIST + L, :] = qk_pre
        acc = qk_pre * wqk[QK_CONV_K - 1:QK_CONV_K, cs] + bqk[:, cs]
        for j in range(QK_CONV_K - 1):
            acc = acc + qkc[QK_BASE + j:QK_BASE + j + L, :] * wqk[j:j + 1, cs]
        qkh[:, cs] = qkc[L:L + QK_HIST, :]
        acc = _silu(acc)
        if it * QK_CHUNK >= D_ML:
            acc = acc * (ML_HEAD_DIM ** -0.5)
        qks[:, cs] = _words(acc.astype(BF16))

    qk_chunk(0)
    qk_chunk(ML_HEADS)

    logf = jax.nn.log_sigmoid(g + fb[...])
    if pad:
        logf = jnp.where(valid, logf, 0.0)
    b_al = pltpu.roll(_cumsum_rows(logf, L), GATE_W - ML_HEADS, axis=1)
    m_prev = m_out[0]
    inter = b_al + m_prev
    d_src = g - b_al
    d_src_t = d_src.T
    tt = lax.broadcasted_iota(jnp.int32, (L, L), 0)
    ss = lax.broadcasted_iota(jnp.int32, (L, L), 1)
    mask = ss <= tt
    if pad:
        mask = mask & ((ss + c * L) >= pad)

    m_cols = []
    for h_i in range(ML_HEADS):
        hs = slice(h_i * ML_HEAD_DIM, (h_i + 1) * ML_HEAD_DIM)
        v = proj(CB_V + h_i * ML_HEAD_DIM, ML_HEAD_DIM)
        og = proj(CB_O + h_i * ML_HEAD_DIM, ML_HEAD_DIM)
        zm = proj(CB_ZM + h_i * ML_HEAD_DIM, ML_HEAD_DIM)
        if h_i + 1 < ML_HEADS:
            qk_chunk(h_i + 1)
            qk_chunk(ML_HEADS + h_i + 1)
        q = _halves(qks[:, hs])
        k = _halves(qks[:, D_ML + h_i * ML_HEAD_DIM:D_ML + (h_i + 1) * ML_HEAD_DIM])
        dmat = jnp.where(mask, b_al[:, h_i:h_i + 1] + d_src_t[h_i:h_i + 1, :], -jnp.inf)
        inter_h = inter[:, h_i:h_i + 1]
        m_t = jnp.maximum(inter_h, jnp.max(dmat, axis=-1, keepdims=True))
        w_intra = jnp.exp(dmat - m_t)
        w_inter = jnp.exp(inter_h - m_t)
        s_mat = lax.dot_general(q, k, (((1,), (1,)), ((), ())), preferred_element_type=F32) * w_intra
        n_old = n_out[0, h_i:h_i + 1, :]
        qc = lax.dot_general(q, _halves(cw[h_i]), (((1,), (1,)), ((), ())), preferred_element_type=F32)
        num = jnp.dot(s_mat.astype(BF16), v.astype(BF16), preferred_element_type=F32) + w_inter * qc
        qn = jnp.sum(q.astype(F32) * n_old, axis=-1, keepdims=True)
        den = jnp.sum(s_mat, axis=-1, keepdims=True) + w_inter * qn
        hh = num * (1.0 / jnp.maximum(jnp.abs(den), jnp.exp(-m_t)))
        hh = hh * lax.rsqrt(jnp.mean(hh * hh, axis=-1, keepdims=True) + EPS) * mln[:, hs]
        hm = hh * _sigmoid(og) * _silu(zm)
        hm_scr[:, hs] = _words(hm.astype(BF16))

        m_new = m_t[L - 1:L, :]
        decay = jnp.exp(b_al[L - 1:L, h_i:h_i + 1] + m_prev[:, h_i:h_i + 1] - m_new)
        w_s = jnp.exp(d_src[:, h_i:h_i + 1] + (b_al[L - 1:L, h_i:h_i + 1] - m_new))
        if pad:
            w_s = jnp.where(valid, w_s, 0.0)
        vw = (w_s * v).astype(BF16)
        c_new = decay * c_out[0, h_i] + lax.dot_general(
            vw, k, (((0,), (0,)), ((), ())), preferred_element_type=F32)
        c_out[0, h_i] = c_new
        cw[h_i] = _words(c_new.astype(BF16))
        n_out[0, h_i:h_i + 1, :] = decay * n_old + jnp.sum(w_s * k.astype(F32), axis=0, keepdims=True)
        m_cols.append(m_new)
    qb_out[0] = qkh[QK_BASE:QK_HIST, :]

    lane = lax.broadcasted_iota(jnp.int32, (1, GATE_W), 1)
    m_row = jnp.zeros((1, GATE_W), F32)
    for h_i in range(ML_HEADS):
        m_row = jnp.where(lane == h_i, m_cols[h_i], m_row)
    m_out[0] = m_row

    if fresh:
        gm = _wdot(_halves(h_scr[...]), wgm[...]) + bgm[...]
    else:
        gm = pgm_ref[0]
    br_m = _wdot(_halves(hm_scr[...]), wmo[...])
    ym = gcb_ref[0] + _sigmoid(gm) * br_m
    out = _wdot(ym.astype(BF16), wo[...])
    y_ref[0] = y_ref[0] + _rms(out, npost[...])


def _mstate_shapes(depth, S):
    return [
        (depth, S, QK_CONV_K - 1, 2 * D_ML),
        (depth, S, ML_HEADS, ML_HEAD_DIM, ML_HEAD_DIM),
        (depth, S, ML_HEADS, ML_HEAD_DIM),
        (depth, S, 1, GATE_W),
    ]


def _mlstm_branch(x, gcb, lw, layer, depth, L, *, meta=None, proj=None, state=None, prev_out=None,
                  pad=0, drop_first=False):
    fresh = proj is None
    first_block = meta is not None
    chained = prev_out is not None
    S, T = gcb.shape[0], gcb.shape[1]
    shapes = _mstate_shapes(depth, S)
    shift = lambda c: jnp.maximum(c - 1, 0)
    blk = lambda s, c: (s, c, 0)
    full2 = lambda s, c: (0, 0)
    const = dict(pipeline_mode=pl.Buffered(1))

    def state_spec(shape, **kw):
        nd = len(shape) - 2
        return pl.BlockSpec((None, 1) + shape[2:], lambda s, c: (layer, s) + (0,) * nd, **kw)

    in_specs = [pl.BlockSpec((1, L, D_MODEL), (lambda s, c: (s, shift(c), 0)) if first_block else blk)]
    args = [x]
    if first_block:
        in_specs.append(pl.BlockSpec(meta.shape, full2, **const))
        args.append(meta)
    in_specs.append(pl.BlockSpec((1, L, D_MODEL), blk))
    args.append(gcb)
    if fresh:
        ws = [lw["norm_pre"], lw["w_pb"], lw["b_pb"], lw["w_gm"], lw["b_gm"], lw["w_gate"], lw["b_gate"]]
    else:
        in_specs += [pl.BlockSpec((1, L, D_PB), blk), pl.BlockSpec((1, L, D_MODEL), blk),
                     pl.BlockSpec((1, L, GATE_W), blk)]
        args += list(proj)
        in_specs += [state_spec(sh) for sh in shapes]
        args += list(state)
        ws = []
    ws += [lw["w_qkc"], lw["b_qkc"], lw["f_bias"], lw["ml_norm"], lw["w_ml_out"], lw["w_out"],
           lw["norm_post"]]
    in_specs += [pl.BlockSpec(w.shape, full2, **const) for w in ws]
    args += ws
    aliases = {}
    if chained:
        for k, a in enumerate(prev_out):
            aliases[len(args)] = 1 + k
            in_specs.append(pl.BlockSpec(memory_space=pl.ANY))
            args.append(a)

    t_out = T - L if drop_first else T
    out_shape = [jax.ShapeDtypeStruct((S, t_out, D_MODEL), F32)] + \
        [jax.ShapeDtypeStruct(sh, F32) for sh in shapes]
    out_specs = [pl.BlockSpec((1, L, D_MODEL), (lambda s, c: (s, shift(c), 0)) if drop_first else blk)] + \
        [state_spec(sh) for sh in shapes]
    scratch = [
        pltpu.VMEM((L + QK_HIST, QK_CHUNK), F32),
        pltpu.VMEM((QK_HIST, 2 * D_ML), F32),
        pltpu.VMEM((L // 2, 2 * D_ML), U32),
        pltpu.VMEM((L // 2, D_ML), U32),
        pltpu.VMEM((ML_HEADS, ML_HEAD_DIM // 2, ML_HEAD_DIM), U32),
    ]
    if fresh:
        scratch.append(pltpu.VMEM((L // 2, D_MODEL), U32))
    outs = pl.pallas_call(
        functools.partial(_mlstm_kernel, L=L, pad=pad, fresh=fresh, first_block=first_block,
                          chained=chained),
        grid=(S, T // L),
        in_specs=in_specs,
        out_specs=out_specs,
        out_shape=out_shape,
        scratch_shapes=scratch,
        input_output_aliases=aliases,
        compiler_params=_params(2),
        name="mlstm_fresh" if fresh else "mlstm_carry",
    )(*args)
    return outs[0], outs[1:]


L_PROMPT = 256
TM_SAMPLE = 1024


def _layer_weights(l, norm_pre, norm_post, w_in, b_in, w_dw, b_dw, ln_g, ln_b, w_qk_conv, b_qk_conv,
                   f_bias, ml_norm, w_conv_out, w_ml_out, w_out):
    b = b_in[l]
    tail = w_in[l, :, LO_IF:]
    w_gate = jnp.pad(tail[:, :N_IF], ((0, 0), (0, GATE_W - N_IF)))[None]
    w_gc = tail[:, N_IF:N_IF + D_MODEL][None]
    w_gm = tail[:, N_IF + D_MODEL:][None]
    row = lambda a: a[None, :]
    return dict(
        norm_pre=row(norm_pre[l]), norm_post=row(norm_post[l]),
        w_pa=_pack(w_in, l, 0, D_PA), b_pa=row(b[:D_PA]),
        w_pb=_pack(w_in, l, D_PA, D_PB), b_pb=row(b[D_PA:LO_IF]),
        w_gate=_pack(w_gate, 0), b_gate=row(jnp.pad(b[LO_IF:LO_GC], (0, GATE_W - N_IF))),
        w_gc=_pack(w_gc, 0), b_gc=row(b[LO_GC:LO_GM]),
        w_gm=_pack(w_gm, 0), b_gm=row(b[LO_GM:]),
        w_dw=w_dw[l], b_dw=row(b_dw[l]), ln_g=row(ln_g[l]), ln_b=row(ln_b[l]),
        w_qkc=w_qk_conv[l], b_qkc=row(b_qk_conv[l]),
        f_bias=row(jnp.pad(f_bias[l], (ML_HEADS, GATE_W - N_IF))), ml_norm=row(ml_norm[l]),
        w_conv_out=_pack(w_conv_out, l), w_ml_out=_pack(w_ml_out, l), w_out=_pack(w_out, l))


def kernel(x_prompt, x_sample, state_conv, state_qk_conv, state_C, state_n, state_m, meta_tokens,
           norm_pre, norm_post, w_in, b_in, w_dw, b_dw, ln_g, ln_b, w_qk_conv, b_qk_conv, f_bias,
           ml_norm, w_conv_out, w_ml_out, w_out):
    bp, seq, _ = x_prompt.shape
    bs, dec_seq, _ = x_sample.shape
    depth = w_in.shape[0]
    L = L_PROMPT
    pad = L - N_META
    assert seq % L == 0 and pad % SUBLANES == 0 and dec_seq % (2 * SUBLANES) == 0

    meta = meta_tokens.astype(x_prompt.dtype)
    m_pad = jnp.pad(state_m, ((0, 0), (0, 0), (0, GATE_W - ML_HEADS)))[:, :, None, :]
    mstate_in = (state_qk_conv, state_C, state_n, m_pad)

    xp, xs = x_prompt, x_sample
    cb_p = cb_s = st_p = st_s = None
    for l in range(depth):
        lw = _layer_weights(l, norm_pre, norm_post, w_in, b_in, w_dw, b_dw, ln_g, ln_b, w_qk_conv,
                            b_qk_conv, f_bias, ml_norm, w_conv_out, w_ml_out, w_out)
        first = meta if l == 0 else None
        gcb, cb_p = _conv_branch(xp, lw, l, depth, L, meta=first, prev_out=cb_p, pad=pad)
        xp, st_p = _mlstm_branch(xp, gcb, lw, l, depth, L, meta=first, prev_out=st_p, pad=pad,
                                 drop_first=(l == depth - 1))

        xs2 = xs.reshape(bs * dec_seq, D_MODEL)
        tok = lambda a: a.reshape(bs, dec_seq, a.shape[-1])
        (pa,) = _inproj(xs2, lw["norm_pre"], lw["w_pa"], lw["b_pa"], TM_SAMPLE)
        (pgc,) = _inproj(xs2, lw["norm_pre"], lw["w_gc"], lw["b_gc"], TM_SAMPLE)
        pb, pg = _inproj(xs2, lw["norm_pre"], lw["w_pb"], lw["b_pb"], TM_SAMPLE, lw["w_gate"], lw["b_gate"])
        (pgm,) = _inproj(xs2, lw["norm_pre"], lw["w_gm"], lw["b_gm"], TM_SAMPLE)
        gcb, cb_s = _conv_branch(None, lw, l, depth, dec_seq, proj=(tok(pa), tok(pgc)),
                                 state=state_conv, prev_out=cb_s)
        xs, st_s = _mlstm_branch(xs, gcb, lw, l, depth, dec_seq, proj=(tok(pb), tok(pgm), tok(pg)),
                                 state=mstate_in, prev_out=st_s)

    fin = lambda cb, st: (cb,) + tuple(st[:3]) + (st[3][:, :, 0, :ML_HEADS],)
    return (xp, xs) + fin(cb_p, st_p) + fin(cb_s, st_s)
```

```python
import functools

import jax
import jax.numpy as jnp
from jax import lax
from jax.experimental import pallas as pl
from jax.experimental.pallas import tpu as pltpu

F32 = jnp.float32
BF16 = jnp.bfloat16
U32 = jnp.uint32

D_MODEL = 1024
D_CONV = D_MODEL
CONV_K = 31
D_ML = 2 * D_MODEL
ML_HEADS = 4
ML_HEAD_DIM = D_ML // ML_HEADS
QK_CONV_K = 4
N_META = 16
EPS = 1e-6

LANES = 128
SUBLANES = 8
GATE_W = LANES
N_IF = 2 * ML_HEADS
CONV_HIST = 32
CONV_BASE = CONV_HIST - (CONV_K - 1)
CONV_SPAN = (CONV_BASE + CONV_K - 2) // SUBLANES * SUBLANES
QK_HIST = SUBLANES
QK_BASE = QK_HIST - (QK_CONV_K - 1)
QK_CHUNK = ML_HEAD_DIM
CONV_ROWS = 128
GLU_CHUNK = 2 * LANES
TN = 1024
VMEM_LIMIT = 62 * 1024 * 1024

D_PA = 3 * D_CONV
D_PB = 5 * D_ML
LO_IF = D_PA + D_PB
LO_GC = LO_IF + N_IF
LO_GM = LO_GC + D_MODEL
CA_A, CA_GL, CA_ZC = 0, D_CONV, 2 * D_CONV
CB_Q, CB_K, CB_V, CB_O, CB_ZM = 0, D_ML, 2 * D_ML, 3 * D_ML, 4 * D_ML


def _sigmoid(x):
    return jax.nn.sigmoid(x)


def _silu(x):
    return x * jax.nn.sigmoid(x)


def _rms(x, g):
    return x * lax.rsqrt(jnp.mean(x * x, axis=-1, keepdims=True) + EPS) * g


def _words(x_bf16):
    return pltpu.bitcast(x_bf16, U32)


def _halves(x_u32):
    return pltpu.bitcast(x_u32, BF16)


def _wdot(x_bf16, w_words):
    return jnp.dot(x_bf16, _halves(w_words), preferred_element_type=F32)


def _params(n_grid):
    return pltpu.CompilerParams(dimension_semantics=("arbitrary",) * n_grid,
                                vmem_limit_bytes=VMEM_LIMIT)


def _pack_kernel(w_ref, o_ref):
    o_ref[...] = _words(w_ref[...].astype(BF16))


def _pack(w3d, layer, col_lo=0, n_col=None):
    _, k, n = w3d.shape
    n_col = n - col_lo if n_col is None else n_col
    tn = min(n_col, TN)
    assert col_lo % tn == 0 and n_col % tn == 0
    j0 = col_lo // tn
    return pl.pallas_call(
        _pack_kernel,
        grid=(n_col // tn,),
        in_specs=[pl.BlockSpec((None, k, tn), lambda j: (layer, 0, j + j0))],
        out_specs=pl.BlockSpec((k // 2, tn), lambda j: (0, j)),
        out_shape=jax.ShapeDtypeStruct((k // 2, n_col), U32),
        compiler_params=_params(1),
        name="pack_weight",
    )(w3d)


def _inproj_kernel(*refs, gated):
    if gated:
        x_ref, g_ref, w_ref, b_ref, wg_ref, bg_ref, o_ref, og_ref, h_ref = refs
    else:
        x_ref, g_ref, w_ref, b_ref, o_ref, h_ref = refs

    @pl.when(pl.program_id(1) == 0)
    def _():
        hb = _rms(x_ref[...], g_ref[...]).astype(BF16)
        h_ref[...] = _words(hb)
        if gated:
            og_ref[...] = _wdot(hb, wg_ref[...]) + bg_ref[...]

    o_ref[...] = _wdot(_halves(h_ref[...]), w_ref[...]) + b_ref[...]


def _inproj(x2d, g, w_words, b, tm, w_gate=None, b_gate=None):
    n_tok, n_col = x2d.shape[0], w_words.shape[1]
    kw = w_words.shape[0]
    gated = w_gate is not None
    in_specs = [
        pl.BlockSpec((tm, D_MODEL), lambda i, j: (i, 0)),
        pl.BlockSpec((1, D_MODEL), lambda i, j: (0, 0)),
        pl.BlockSpec((kw, TN), lambda i, j: (0, j)),
        pl.BlockSpec((1, TN), lambda i, j: (0, j)),
    ]
    out_specs = [pl.BlockSpec((tm, TN), lambda i, j: (i, j))]
    out_shape = [jax.ShapeDtypeStruct((n_tok, n_col), F32)]
    args = [x2d, g, w_words, b]
    if gated:
        in_specs += [pl.BlockSpec((kw, GATE_W), lambda i, j: (0, 0)),
                     pl.BlockSpec((1, GATE_W), lambda i, j: (0, 0))]
        out_specs.append(pl.BlockSpec((tm, GATE_W), lambda i, j: (i, 0)))
        out_shape.append(jax.ShapeDtypeStruct((n_tok, GATE_W), F32))
        args += [w_gate, b_gate]
    return pl.pallas_call(
        functools.partial(_inproj_kernel, gated=gated),
        grid=(n_tok // tm, n_col // TN),
        in_specs=in_specs, out_specs=out_specs, out_shape=out_shape,
        scratch_shapes=[pltpu.VMEM((tm // 2, D_MODEL), U32)],
        compiler_params=_params(2),
        name="inproj",
    )(*args)


def _cumsum_rows(x, n_rows):
    row = lax.broadcasted_iota(jnp.int32, x.shape, 0)
    k = 1
    while k < n_rows:
        x = x + jnp.where(row >= k, pltpu.roll(x, k, axis=0), 0.0)
        k *= 2
    return x


def _conv_kernel(*refs, L, pad, fresh, first_block, chained):
    refs = list(refs)
    x_ref = refs.pop(0) if fresh else None
    meta_ref = refs.pop(0) if first_block else None
    if fresh:
        npre, wpa, bpa, wgc, bgc = (refs.pop(0) for _ in range(5))
    else:
        pa_ref, pgc_ref, cb_in = (refs.pop(0) for _ in range(3))
    wdw, bdw, lng, lnb, wco = (refs.pop(0) for _ in range(5))
    if chained:
        refs.pop(0)
    gcb_ref, cb_out = refs.pop(0), refs.pop(0)
    ubuf, ush, cvo = (refs.pop(0) for _ in range(3))

    c = pl.program_id(1)

    @pl.when(c == 0)
    def _init():
        ubuf[0:CONV_HIST, :] = jnp.zeros((CONV_HIST, D_CONV), F32)
        if not fresh:
            ubuf[CONV_BASE:CONV_HIST, :] = cb_in[0]

    if fresh:
        x = x_ref[0]
        if first_block:
            x0 = jnp.concatenate([jnp.zeros((pad, D_MODEL), F32), meta_ref[...]], axis=0)
            x = jnp.where(c == 0, x0, x)
        h = _rms(x, npre[...]).astype(BF16)

        def proj_a(off, width):
            return _wdot(h, wpa[:, off:off + width]) + bpa[:, off:off + width]

        def proj_gc(off, width):
            return _wdot(h, wgc[:, off:off + width]) + bgc[:, off:off + width]
    else:
        def proj_a(off, width):
            return pa_ref[0, :, off:off + width]

        def proj_gc(off, width):
            return pgc_ref[0, :, off:off + width]

    if pad:
        row_id = lax.broadcasted_iota(jnp.int32, (L, 1), 0) + c * L
        valid = row_id >= pad

    rc = min(L, CONV_ROWS)
    n_chunk = D_CONV // GLU_CHUNK
    for ck in range(n_chunk):
        co = ck * GLU_CHUNK
        u = proj_a(CA_A + co, GLU_CHUNK) * _sigmoid(proj_a(CA_GL + co, GLU_CHUNK))
        if pad:
            u = jnp.where(valid, u, 0.0)
        ubuf[CONV_HIST:CONV_HIST + L, co:co + GLU_CHUNK] = u
        if ck == n_chunk - 2:
            zc = proj_a(CA_ZC, D_CONV)
        if ck == n_chunk - 1:
            gc = proj_gc(0, D_MODEL)
        for it in range(co // LANES, (co + GLU_CHUNK) // LANES):
            ls = slice(it * LANES, (it + 1) * LANES)
            sb = (it % 2) * (SUBLANES - 1)
            for sh in range(1, SUBLANES):
                ush[sb + sh - 1, :, :] = ubuf[sh:sh + L + CONV_SPAN, ls]
            for r0 in range(0, L, rc):
                acc = jnp.zeros((rc, LANES), F32) + bdw[:, ls]
                for j in range(CONV_K):
                    a8, sh = divmod(CONV_BASE + j, SUBLANES)
                    lo = a8 * SUBLANES + r0
                    src = ubuf[lo:lo + rc, ls] if sh == 0 else ush[sb + sh - 1, lo:lo + rc, :]
                    acc = acc + src * wdw[j:j + 1, ls]
                cvo[r0:r0 + rc, ls] = acc
    cb_out[0] = ubuf[L + CONV_BASE:L + CONV_HIST, :]
    ubuf[0:CONV_HIST, :] = ubuf[L:L + CONV_HIST, :]

    cv = cvo[...]
    mu = jnp.mean(cv, axis=-1, keepdims=True)
    xc = cv - mu
    ln = xc * lax.rsqrt(jnp.mean(xc * xc, axis=-1, keepdims=True) + EPS) * lng[...] + lnb[...]
    ua = _silu(ln) * _silu(zc)
    br_c = _wdot(ua.astype(BF16), wco[...])
    gcb_ref[0] = _sigmoid(gc) * br_c


def _conv_branch(x, lw, layer, depth, L, *, meta=None, proj=None, state=None, prev_out=None, pad=0):
    fresh = proj is None
    first_block = meta is not None
    chained = prev_out is not None
    src = x if fresh else proj[0]
    S = src.shape[0]
    T = src.shape[1] + (L if first_block else 0)
    shift = lambda c: jnp.maximum(c - 1, 0)
    blk = lambda s, c: (s, c, 0)
    full2 = lambda s, c: (0, 0)
    const = dict(pipeline_mode=pl.Buffered(1))
    st_shape = (depth, S, CONV_K - 1, D_CONV)
    st_spec = pl.BlockSpec((None, 1) + st_shape[2:], lambda s, c: (layer, s, 0, 0))

    in_specs, args = [], []
    if fresh:
        in_specs.append(pl.BlockSpec((1, L, D_MODEL), (lambda s, c: (s, shift(c), 0)) if first_block
                                     else blk))
        args.append(x)
        if first_block:
            in_specs.append(pl.BlockSpec(meta.shape, full2, **const))
            args.append(meta)
        ws = [lw["norm_pre"], lw["w_pa"], lw["b_pa"], lw["w_gc"], lw["b_gc"]]
    else:
        in_specs += [pl.BlockSpec((1, L, D_PA), blk), pl.BlockSpec((1, L, D_MODEL), blk), st_spec]
        args += [proj[0], proj[1], state]
        ws = []
    ws += [lw["w_dw"], lw["b_dw"], lw["ln_g"], lw["ln_b"], lw["w_conv_out"]]
    in_specs += [pl.BlockSpec(w.shape, full2, **const) for w in ws]
    args += ws
    aliases = {}
    if chained:
        aliases[len(args)] = 1
        in_specs.append(pl.BlockSpec(memory_space=pl.ANY))
        args.append(prev_out)

    scratch = [
        pltpu.VMEM((L + CONV_HIST, D_CONV), F32),
        pltpu.VMEM((2 * (SUBLANES - 1), L + CONV_SPAN, LANES), F32),
        pltpu.VMEM((L, D_CONV), F32),
    ]
    return pl.pallas_call(
        functools.partial(_conv_kernel, L=L, pad=pad, fresh=fresh, first_block=first_block,
                          chained=chained),
        grid=(S, T // L),
        in_specs=in_specs,
        out_specs=[pl.BlockSpec((1, L, D_MODEL), blk), st_spec],
        out_shape=[jax.ShapeDtypeStruct((S, T, D_MODEL), F32), jax.ShapeDtypeStruct(st_shape, F32)],
        scratch_shapes=scratch,
        input_output_aliases=aliases,
        compiler_params=_params(2),
        name="conv_fresh" if fresh else "conv_carry",
    )(*args)


N_MSTATE = 4


def _mlstm_kernel(*refs, L, pad, fresh, first_block, chained):
    refs = list(refs)
    x_ref = refs.pop(0)
    meta_ref = refs.pop(0) if first_block else None
    gcb_ref = refs.pop(0)
    if fresh:
        npre, wpb, bpb, wgm, bgm, wgate, bgate = (refs.pop(0) for _ in range(7))
    else:
        pb_ref, pgm_ref, pg_ref = (refs.pop(0) for _ in range(3))
        qb_in, c_in, n_in, m_in = (refs.pop(0) for _ in range(N_MSTATE))
    wqk, bqk, fb, mln, wmo, wo, npost = (refs.pop(0) for _ in range(7))
    if chained:
        del refs[:N_MSTATE]
    y_ref, qb_out, c_out, n_out, m_out = (refs.pop(0) for _ in range(1 + N_MSTATE))
    qkc, qkh, qks, cw = (refs.pop(0) for _ in range(4))

    c = pl.program_id(1)

    @pl.when(c == 0)
    def _init():
        qkh[...] = jnp.zeros(qkh.shape, F32)
        if fresh:
            c_out[...] = jnp.zeros(c_out.shape, F32)
            n_out[...] = jnp.zeros(n_out.shape, F32)
            m_out[...] = jnp.zeros(m_out.shape, F32)
            cw[...] = jnp.zeros(cw.shape, U32)
        else:
            qkh[QK_BASE:QK_HIST, :] = qb_in[0]
            c_out[...] = c_in[...]
            n_out[...] = n_in[...]
            m_out[...] = m_in[...]
            for h_i in range(ML_HEADS):
                cw[h_i] = _words(c_in[0, h_i].astype(BF16))

    def x_block():
        x = x_ref[0]
        if first_block:
            x0 = jnp.concatenate([jnp.zeros((pad, D_MODEL), F32), meta_ref[...]], axis=0)
            x = jnp.where(c == 0, x0, x)
        return x

    if fresh:
        h = _rms(x_block(), npre[...]).astype(BF16)

        def proj(off, width):
            return _wdot(h, wpb[:, off:off + width]) + bpb[:, off:off + width]

        g = _wdot(h, wgate[...]) + bgate[...]
    else:
        def proj(off, width):
            return pb_ref[0, :, off:off + width]

        g = pg_ref[0]

    row_id = lax.broadcasted_iota(jnp.int32, (L, 1), 0) + c * L
    valid = row_id >= pad

    def qk_chunk(it):
        cs = slice(it * QK_CHUNK, (it + 1) * QK_CHUNK)
        qk_pre = proj(CB_Q + it * QK_CHUNK, QK_CHUNK)
        if pad:
            qk_pre = jnp.where(valid, qk_pre, 0.0)
        qkc[0:QK_HIST, :] = qkh[:, cs]
        qkc[QK_HIST:QK_HIST + L, :] = qk_pre
        acc = qk_pre * wqk[QK_CONV_K - 1:QK_CONV_K, cs] + bqk[:, cs]
        for j in range(QK_CONV_K - 1):
            acc = acc + qkc[QK_BASE + j:QK_BASE + j + L, :] * wqk[j:j + 1, cs]
        qkh[:, cs] = qkc[L:L + QK_HIST, :]
        acc = _silu(acc)
        if it * QK_CHUNK >= D_ML:
            acc = acc * (ML_HEAD_DIM ** -0.5)
        qks[:, cs] = _words(acc.astype(BF16))

    for it in range(2 * D_ML // QK_CHUNK):
        qk_chunk(it)
    qb_out[0] = qkh[QK_BASE:QK_HIST, :]

    logf = jax.nn.log_sigmoid(g + fb[...])
    if pad:
        logf = jnp.where(valid, logf, 0.0)
    b_al = pltpu.roll(_cumsum_rows(logf, L), GATE_W - ML_HEADS, axis=1)
    m_prev = m_out[0]
    inter = b_al + m_prev
    d_src = g - b_al
    d_src_t = d_src.T
    tt = lax.broadcasted_iota(jnp.int32, (L, L), 0)
    ss = lax.broadcasted_iota(jnp.int32, (L, L), 1)
    mask = ss <= tt
    if pad:
        mask = mask & ((ss + c * L) >= pad)

    hms, updates = [], []
    for h_i in range(ML_HEADS):
        hs = slice(h_i * ML_HEAD_DIM, (h_i + 1) * ML_HEAD_DIM)
        v = proj(CB_V + h_i * ML_HEAD_DIM, ML_HEAD_DIM)
        og = proj(CB_O + h_i * ML_HEAD_DIM, ML_HEAD_DIM)
        zm = proj(CB_ZM + h_i * ML_HEAD_DIM, ML_HEAD_DIM)
        q = _halves(qks[:, hs])
        k = _halves(qks[:, D_ML + h_i * ML_HEAD_DIM:D_ML + (h_i + 1) * ML_HEAD_DIM])
        dmat = jnp.where(mask, b_al[:, h_i:h_i + 1] + d_src_t[h_i:h_i + 1, :], -jnp.inf)
        inter_h = inter[:, h_i:h_i + 1]
        m_t = jnp.maximum(inter_h, jnp.max(dmat, axis=-1, keepdims=True))
        w_intra = jnp.exp(dmat - m_t)
        w_inter = jnp.exp(inter_h - m_t)
        s_mat = lax.dot_general(q, k, (((1,), (1,)), ((), ())), preferred_element_type=F32) * w_intra
        n_old = n_out[0, h_i:h_i + 1, :]
        qc = lax.dot_general(q, _halves(cw[h_i]), (((1,), (1,)), ((), ())), preferred_element_type=F32)
        num = jnp.dot(s_mat.astype(BF16), v.astype(BF16), preferred_element_type=F32) + w_inter * qc
        qn = jnp.sum(q.astype(F32) * n_old, axis=-1, keepdims=True)
        den = jnp.sum(s_mat, axis=-1, keepdims=True) + w_inter * qn
        hh = num * (1.0 / jnp.maximum(jnp.abs(den), jnp.exp(-m_t)))
        hh = hh * lax.rsqrt(jnp.mean(hh * hh, axis=-1, keepdims=True) + EPS) * mln[:, hs]
        hm = hh * _sigmoid(og) * _silu(zm)
        hms.append(hm.astype(BF16))

        m_new = m_t[L - 1:L, :]
        decay = jnp.exp(b_al[L - 1:L, h_i:h_i + 1] + m_prev[:, h_i:h_i + 1] - m_new)
        w_s = jnp.exp(d_src[:, h_i:h_i + 1] + (b_al[L - 1:L, h_i:h_i + 1] - m_new))
        if pad:
            w_s = jnp.where(valid, w_s, 0.0)
        vw = (w_s * v).astype(BF16)
        n_new = decay * n_old + jnp.sum(w_s * k.astype(F32), axis=0, keepdims=True)
        updates.append((m_new, decay, vw, k, n_new))

    if fresh:
        gm = _wdot(h, wgm[...]) + bgm[...]
    else:
        gm = pgm_ref[0]
    br_m = _wdot(jnp.concatenate(hms, axis=1), wmo[...])
    ym = gcb_ref[0] + _sigmoid(gm) * br_m
    out = _wdot(ym.astype(BF16), wo[...])
    y_ref[0] = x_block() + _rms(out, npost[...])

    lane = lax.broadcasted_iota(jnp.int32, (1, GATE_W), 1)
    m_row = jnp.zeros((1, GATE_W), F32)
    for h_i, (m_new, decay, vw, k, n_new) in enumerate(updates):
        c_new = decay * c_out[0, h_i] + lax.dot_general(
            vw, k, (((0,), (0,)), ((), ())), preferred_element_type=F32)
        c_out[0, h_i] = c_new
        cw[h_i] = _words(c_new.astype(BF16))
        n_out[0, h_i:h_i + 1, :] = n_new
        m_row = jnp.where(lane == h_i, m_new, m_row)
    m_out[0] = m_row


def _mstate_shapes(depth, S):
    return [
        (depth, S, QK_CONV_K - 1, 2 * D_ML),
        (depth, S, ML_HEADS, ML_HEAD_DIM, ML_HEAD_DIM),
        (depth, S, ML_HEADS, ML_HEAD_DIM),
        (depth, S, 1, GATE_W),
    ]


def _mlstm_branch(x, gcb, lw, layer, depth, L, *, meta=None, proj=None, state=None, prev_out=None,
                  pad=0, drop_first=False):
    fresh = proj is None
    first_block = meta is not None
    chained = prev_out is not None
    S, T = gcb.shape[0], gcb.shape[1]
    shapes = _mstate_shapes(depth, S)
    shift = lambda c: jnp.maximum(c - 1, 0)
    blk = lambda s, c: (s, c, 0)
    full2 = lambda s, c: (0, 0)
    const = dict(pipeline_mode=pl.Buffered(1))

    def state_spec(shape, **kw):
        nd = len(shape) - 2
        return pl.BlockSpec((None, 1) + shape[2:], lambda s, c: (layer, s) + (0,) * nd, **kw)

    in_specs = [pl.BlockSpec((1, L, D_MODEL), (lambda s, c: (s, shift(c), 0)) if first_block else blk)]
    args = [x]
    if first_block:
        in_specs.append(pl.BlockSpec(meta.shape, full2, **const))
        args.append(meta)
    in_specs.append(pl.BlockSpec((1, L, D_MODEL), blk))
    args.append(gcb)
    if fresh:
        ws = [lw["norm_pre"], lw["w_pb"], lw["b_pb"], lw["w_gm"], lw["b_gm"], lw["w_gate"], lw["b_gate"]]
    else:
        in_specs += [pl.BlockSpec((1, L, D_PB), blk), pl.BlockSpec((1, L, D_MODEL), blk),
                     pl.BlockSpec((1, L, GATE_W), blk)]
        args += list(proj)
        in_specs += [state_spec(sh) for sh in shapes]
        args += list(state)
        ws = []
    ws += [lw["w_qkc"], lw["b_qkc"], lw["f_bias"], lw["ml_norm"], lw["w_ml_out"], lw["w_out"],
           lw["norm_post"]]
    in_specs += [pl.BlockSpec(w.shape, full2, **const) for w in ws]
    args += ws
    aliases = {}
    if chained:
        for k, a in enumerate(prev_out):
            aliases[len(args)] = 1 + k
            in_specs.append(pl.BlockSpec(memory_space=pl.ANY))
            args.append(a)

    t_out = T - L if drop_first else T
    out_shape = [jax.ShapeDtypeStruct((S, t_out, D_MODEL), F32)] + \
        [jax.ShapeDtypeStruct(sh, F32) for sh in shapes]
    out_specs = [pl.BlockSpec((1, L, D_MODEL), (lambda s, c: (s, shift(c), 0)) if drop_first else blk)] + \
        [state_spec(sh) for sh in shapes]
    scratch = [
        pltpu.VMEM((L + QK_HIST, QK_CHUNK), F32),
        pltpu.VMEM((QK_HIST, 2 * D_ML), F32),
        pltpu.VMEM((L // 2, 2 * D_ML), U32),
        pltpu.VMEM((ML_HEADS, ML_HEAD_DIM // 2, ML_HEAD_DIM), U32),
    ]
    outs = pl.pallas_call(
        functools.partial(_mlstm_kernel, L=L, pad=pad, fresh=fresh, first_block=first_block,
                          chained=chained),
        grid=(S, T // L),
        in_specs=in_specs,
        out_specs=out_specs,
        out_shape=out_shape,
        scratch_shapes=scratch,
        input_output_aliases=aliases,
        compiler_params=_params(2),
        name="mlstm_fresh" if fresh else "mlstm_carry",
    )(*args)
    return outs[0], outs[1:]


L_PROMPT = 256
TM_SAMPLE = 1024


def _layer_weights(l, norm_pre, norm_post, w_in, b_in, w_dw, b_dw, ln_g, ln_b, w_qk_conv, b_qk_conv,
                   f_bias, ml_norm, w_conv_out, w_ml_out, w_out):
    b = b_in[l]
    tail = w_in[l, :, LO_IF:]
    w_gate = jnp.pad(tail[:, :N_IF], ((0, 0), (0, GATE_W - N_IF)))[None]
    w_gc = tail[:, N_IF:N_IF + D_MODEL][None]
    w_gm = tail[:, N_IF + D_MODEL:][None]
    row = lambda a: a[None, :]
    return dict(
        norm_pre=row(norm_pre[l]), norm_post=row(norm_post[l]),
        w_pa=_pack(w_in, l, 0, D_PA), b_pa=row(b[:D_PA]),
        w_pb=_pack(w_in, l, D_PA, D_PB), b_pb=row(b[D_PA:LO_IF]),
        w_gate=_pack(w_gate, 0), b_gate=row(jnp.pad(b[LO_IF:LO_GC], (0, GATE_W - N_IF))),
        w_gc=_pack(w_gc, 0), b_gc=row(b[LO_GC:LO_GM]),
        w_gm=_pack(w_gm, 0), b_gm=row(b[LO_GM:]),
        w_dw=w_dw[l], b_dw=row(b_dw[l]), ln_g=row(ln_g[l]), ln_b=row(ln_b[l]),
        w_qkc=w_qk_conv[l], b_qkc=row(b_qk_conv[l]),
        f_bias=row(jnp.pad(f_bias[l], (ML_HEADS, GATE_W - N_IF))), ml_norm=row(ml_norm[l]),
        w_conv_out=_pack(w_conv_out, l), w_ml_out=_pack(w_ml_out, l), w_out=_pack(w_out, l))


def kernel(x_prompt, x_sample, state_conv, state_qk_conv, state_C, state_n, state_m, meta_tokens,
           norm_pre, norm_post, w_in, b_in, w_dw, b_dw, ln_g, ln_b, w_qk_conv, b_qk_conv, f_bias,
           ml_norm, w_conv_out, w_ml_out, w_out):
    bp, seq, _ = x_prompt.shape
    bs, dec_seq, _ = x_sample.shape
    depth = w_in.shape[0]
    L = L_PROMPT
    pad = L - N_META
    assert seq % L == 0 and pad % SUBLANES == 0 and dec_seq % (2 * SUBLANES) == 0

    meta = meta_tokens.astype(x_prompt.dtype)
    m_pad = jnp.pad(state_m, ((0, 0), (0, 0), (0, GATE_W - ML_HEADS)))[:, :, None, :]
    mstate_in = (state_qk_conv, state_C, state_n, m_pad)

    xp, xs = x_prompt, x_sample
    cb_p = cb_s = st_p = st_s = None
    for l in range(depth):
        lw = _layer_weights(l, norm_pre, norm_post, w_in, b_in, w_dw, b_dw, ln_g, ln_b, w_qk_conv,
                            b_qk_conv, f_bias, ml_norm, w_conv_out, w_ml_out, w_out)
        first = meta if l == 0 else None
        gcb, cb_p = _conv_branch(xp, lw, l, depth, L, meta=first, prev_out=cb_p, pad=pad)
        xp, st_p = _mlstm_branch(xp, gcb, lw, l, depth, L, meta=first, prev_out=st_p, pad=pad,
                                 drop_first=(l == depth - 1))

        xs2 = xs.reshape(bs * dec_seq, D_MODEL)
        tok = lambda a: a.reshape(bs, dec_seq, a.shape[-1])
        (pa,) = _inproj(xs2, lw["norm_pre"], lw["w_pa"], lw["b_pa"], TM_SAMPLE)
        (pgc,) = _inproj(xs2, lw["norm_pre"], lw["w_gc"], lw["b_gc"], TM_SAMPLE)
        pb, pg = _inproj(xs2, lw["norm_pre"], lw["w_pb"], lw["b_pb"], TM_SAMPLE, lw["w_gate"], lw["b_gate"])
        (pgm,) = _inproj(xs2, lw["norm_pre"], lw["w_gm"], lw["b_gm"], TM_SAMPLE)
        gcb, cb_s = _conv_branch(None, lw, l, depth, dec_seq, proj=(tok(pa), tok(pgc)),
                                 state=state_conv, prev_out=cb_s)
        xs, st_s = _mlstm_branch(xs, gcb, lw, l, depth, dec_seq, proj=(tok(pb), tok(pgm), tok(pg)),
                                 state=mstate_in, prev_out=st_s)

    fin = lambda cb, st: (cb,) + tuple(st[:3]) + (st[3][:, :, 0, :ML_HEADS],)
    return (xp, xs) + fin(cb_p, st_p) + fin(cb_s, st_s)
```

```python
import functools

import jax
import jax.numpy as jnp
from jax import lax
from jax.experimental import pallas as pl
from jax.experimental.pallas import tpu as pltpu

F32 = jnp.float32
BF16 = jnp.bfloat16
U32 = jnp.uint32

D_MODEL = 1024
D_CONV = D_MODEL
CONV_K = 31
D_ML = 2 * D_MODEL
ML_HEADS = 4
ML_HEAD_DIM = D_ML // ML_HEADS
QK_CONV_K = 4
N_META = 16
EPS = 1e-6

LANES = 128
SUBLANES = 8
GATE_W = LANES
N_IF = 2 * ML_HEADS
CONV_HIST = 32
CONV_BASE = CONV_HIST - (CONV_K - 1)
CONV_SPAN = (CONV_BASE + CONV_K - 2) // SUBLANES * SUBLANES
QK_HIST = SUBLANES
QK_BASE = QK_HIST - (QK_CONV_K - 1)
QK_CHUNK = ML_HEAD_DIM
CONV_ROWS = 128
GLU_CHUNK = 2 * LANES
TN = 1024
VMEM_LIMIT = 62 * 1024 * 1024

D_PA = 3 * D_CONV
D_PB = 5 * D_ML
LO_IF = D_PA + D_PB
LO_GC = LO_IF + N_IF
LO_GM = LO_GC + D_MODEL
CA_A, CA_GL, CA_ZC = 0, D_CONV, 2 * D_CONV
CB_Q, CB_K, CB_V, CB_O, CB_ZM = 0, D_ML, 2 * D_ML, 3 * D_ML, 4 * D_ML


def _sigmoid(x):
    return jax.nn.sigmoid(x)


def _silu(x):
    return x * jax.nn.sigmoid(x)


def _rms(x, g):
    return x * lax.rsqrt(jnp.mean(x * x, axis=-1, keepdims=True) + EPS) * g


def _words(x_bf16):
    return pltpu.bitcast(x_bf16, U32)


def _halves(x_u32):
    return pltpu.bitcast(x_u32, BF16)


def _wdot(x_bf16, w_words):
    return jnp.dot(x_bf16, _halves(w_words), preferred_element_type=F32)


def _params(n_grid):
    return pltpu.CompilerParams(dimension_semantics=("arbitrary",) * n_grid,
                                vmem_limit_bytes=VMEM_LIMIT)


def _pack_kernel(w_ref, o_ref):
    o_ref[...] = _words(w_ref[...].astype(BF16))


def _pack(w3d, layer, col_lo=0, n_col=None):
    depth, k, n = w3d.shape
    n_col = n - col_lo if n_col is None else n_col
    tn = min(n_col, TN)
    assert col_lo % tn == 0 and n_col % tn == 0
    j0 = col_lo // tn
    return pl.pallas_call(
        _pack_kernel,
        grid=(n_col // tn,),
        in_specs=[pl.BlockSpec((k, tn), lambda j: (layer, j + j0))],
        out_specs=pl.BlockSpec((k // 2, tn), lambda j: (0, j)),
        out_shape=jax.ShapeDtypeStruct((k // 2, n_col), U32),
        compiler_params=_params(1),
        name="pack_weight",
    )(w3d.reshape(depth * k, n))


def _inproj_kernel(*refs, gated):
    if gated:
        x_ref, g_ref, w_ref, b_ref, wg_ref, bg_ref, o_ref, og_ref, h_ref = refs
    else:
        x_ref, g_ref, w_ref, b_ref, o_ref, h_ref = refs

    @pl.when(pl.program_id(1) == 0)
    def _():
        hb = _rms(x_ref[...], g_ref[...]).astype(BF16)
        h_ref[...] = _words(hb)
        if gated:
            og_ref[...] = _wdot(hb, wg_ref[...]) + bg_ref[...]

    o_ref[...] = _wdot(_halves(h_ref[...]), w_ref[...]) + b_ref[...]


def _inproj(x2d, g, w_words, b, tm, w_gate=None, b_gate=None):
    n_tok, n_col = x2d.shape[0], w_words.shape[1]
    kw = w_words.shape[0]
    gated = w_gate is not None
    in_specs = [
        pl.BlockSpec((tm, D_MODEL), lambda i, j: (i, 0)),
        pl.BlockSpec((1, D_MODEL), lambda i, j: (0, 0)),
        pl.BlockSpec((kw, TN), lambda i, j: (0, j)),
        pl.BlockSpec((1, TN), lambda i, j: (0, j)),
    ]
    out_specs = [pl.BlockSpec((tm, TN), lambda i, j: (i, j))]
    out_shape = [jax.ShapeDtypeStruct((n_tok, n_col), F32)]
    args = [x2d, g, w_words, b]
    if gated:
        in_specs += [pl.BlockSpec((kw, GATE_W), lambda i, j: (0, 0)),
                     pl.BlockSpec((1, GATE_W), lambda i, j: (0, 0))]
        out_specs.append(pl.BlockSpec((tm, GATE_W), lambda i, j: (i, 0)))
        out_shape.append(jax.ShapeDtypeStruct((n_tok, GATE_W), F32))
        args += [w_gate, b_gate]
    return pl.pallas_call(
        functools.partial(_inproj_kernel, gated=gated),
        grid=(n_tok // tm, n_col // TN),
        in_specs=in_specs, out_specs=out_specs, out_shape=out_shape,
        scratch_shapes=[pltpu.VMEM((tm // 2, D_MODEL), U32)],
        compiler_params=_params(2),
        name="inproj",
    )(*args)


def _cumsum_rows(x, n_rows):
    row = lax.broadcasted_iota(jnp.int32, x.shape, 0)
    k = 1
    while k < n_rows:
        x = x + jnp.where(row >= k, pltpu.roll(x, k, axis=0), 0.0)
        k *= 2
    return x


def _conv_kernel(*refs, L, pad, fresh, first_block, chained):
    refs = list(refs)
    x_ref = refs.pop(0) if fresh else None
    meta_ref = refs.pop(0) if first_block else None
    if fresh:
        npre, wpa, bpa, wgc, bgc = (refs.pop(0) for _ in range(5))
    else:
        pa_ref, pgc_ref, cb_in = (refs.pop(0) for _ in range(3))
    wdw, bdw, lng, lnb, wco = (refs.pop(0) for _ in range(5))
    if chained:
        refs.pop(0)
    gcb_ref, cb_out = refs.pop(0), refs.pop(0)
    ubuf, ush, cvo = (refs.pop(0) for _ in range(3))

    c = pl.program_id(1)

    @pl.when(c == 0)
    def _init():
        ubuf[0:CONV_HIST, :] = jnp.zeros((CONV_HIST, D_CONV), F32)
        if not fresh:
            ubuf[CONV_BASE:CONV_HIST, :] = cb_in[0]

    if fresh:
        x = x_ref[0]
        if first_block:
            x0 = jnp.concatenate([jnp.zeros((pad, D_MODEL), F32), meta_ref[...]], axis=0)
            x = jnp.where(c == 0, x0, x)
        h = _rms(x, npre[...]).astype(BF16)

        def proj_a(off, width):
            return _wdot(h, wpa[:, off:off + width]) + bpa[:, off:off + width]

        def proj_gc(off, width):
            return _wdot(h, wgc[:, off:off + width]) + bgc[:, off:off + width]
    else:
        def proj_a(off, width):
            return pa_ref[0, :, off:off + width]

        def proj_gc(off, width):
            return pgc_ref[0, :, off:off + width]

    if pad:
        row_id = lax.broadcasted_iota(jnp.int32, (L, 1), 0) + c * L
        valid = row_id >= pad

    rc = min(L, CONV_ROWS)
    n_chunk = D_CONV // GLU_CHUNK
    for ck in range(n_chunk):
        co = ck * GLU_CHUNK
        u = proj_a(CA_A + co, GLU_CHUNK) * _sigmoid(proj_a(CA_GL + co, GLU_CHUNK))
        if pad:
            u = jnp.where(valid, u, 0.0)
        ubuf[CONV_HIST:CONV_HIST + L, co:co + GLU_CHUNK] = u
        if ck == n_chunk - 2:
            zc = proj_a(CA_ZC, D_CONV)
        if ck == n_chunk - 1:
            gc = proj_gc(0, D_MODEL)
        for it in range(co // LANES, (co + GLU_CHUNK) // LANES):
            ls = slice(it * LANES, (it + 1) * LANES)
            sb = (it % 2) * (SUBLANES - 1)
            for sh in range(1, SUBLANES):
                ush[sb + sh - 1, :, :] = ubuf[sh:sh + L + CONV_SPAN, ls]
            for r0 in range(0, L, rc):
                acc = jnp.zeros((rc, LANES), F32) + bdw[:, ls]
                for j in range(CONV_K):
                    a8, sh = divmod(CONV_BASE + j, SUBLANES)
                    lo = a8 * SUBLANES + r0
                    src = ubuf[lo:lo + rc, ls] if sh == 0 else ush[sb + sh - 1, lo:lo + rc, :]
                    acc = acc + src * wdw[j:j + 1, ls]
                cvo[r0:r0 + rc, ls] = acc
    cb_out[0] = ubuf[L + CONV_BASE:L + CONV_HIST, :]
    ubuf[0:CONV_HIST, :] = ubuf[L:L + CONV_HIST, :]

    cv = cvo[...]
    mu = jnp.mean(cv, axis=-1, keepdims=True)
    xc = cv - mu
    ln = xc * lax.rsqrt(jnp.mean(xc * xc, axis=-1, keepdims=True) + EPS) * lng[...] + lnb[...]
    ua = _silu(ln) * _silu(zc)
    br_c = _wdot(ua.astype(BF16), wco[...])
    gcb_ref[0] = _sigmoid(gc) * br_c


def _conv_branch(x, lw, layer, depth, L, *, meta=None, proj=None, state=None, prev_out=None, pad=0):
    fresh = proj is None
    first_block = meta is not None
    chained = prev_out is not None
    src = x if fresh else proj[0]
    S = src.shape[0]
    T = src.shape[1] + (L if first_block else 0)
    shift = lambda c: jnp.maximum(c - 1, 0)
    blk = lambda s, c: (s, c, 0)
    full2 = lambda s, c: (0, 0)
    const = dict(pipeline_mode=pl.Buffered(1))
    st_shape = (depth, S, CONV_K - 1, D_CONV)
    st_spec = pl.BlockSpec((None, 1) + st_shape[2:], lambda s, c: (layer, s, 0, 0))

    in_specs, args = [], []
    if fresh:
        in_specs.append(pl.BlockSpec((1, L, D_MODEL), (lambda s, c: (s, shift(c), 0)) if first_block
                                     else blk))
        args.append(x)
        if first_block:
            in_specs.append(pl.BlockSpec(meta.shape, full2, **const))
            args.append(meta)
        ws = [lw["norm_pre"], lw["w_pa"], lw["b_pa"], lw["w_gc"], lw["b_gc"]]
    else:
        in_specs += [pl.BlockSpec((1, L, D_PA), blk), pl.BlockSpec((1, L, D_MODEL), blk), st_spec]
        args += [proj[0], proj[1], state]
        ws = []
    ws += [lw["w_dw"], lw["b_dw"], lw["ln_g"], lw["ln_b"], lw["w_conv_out"]]
    in_specs += [pl.BlockSpec(w.shape, full2, **const) for w in ws]
    args += ws
    aliases = {}
    if chained:
        aliases[len(args)] = 1
        in_specs.append(pl.BlockSpec(memory_space=pl.ANY))
        args.append(prev_out)

    scratch = [
        pltpu.VMEM((L + CONV_HIST, D_CONV), F32),
        pltpu.VMEM((2 * (SUBLANES - 1), L + CONV_SPAN, LANES), F32),
        pltpu.VMEM((L, D_CONV), F32),
    ]
    return pl.pallas_call(
        functools.partial(_conv_kernel, L=L, pad=pad, fresh=fresh, first_block=first_block,
                          chained=chained),
        grid=(S, T // L),
        in_specs=in_specs,
        out_specs=[pl.BlockSpec((1, L, D_MODEL), blk), st_spec],
        out_shape=[jax.ShapeDtypeStruct((S, T, D_MODEL), F32), jax.ShapeDtypeStruct(st_shape, F32)],
        scratch_shapes=scratch,
        input_output_aliases=aliases,
        compiler_params=_params(2),
        name="conv_fresh" if fresh else "conv_carry",
    )(*args)


N_MSTATE = 4


def _mlstm_kernel(*refs, L, pad, fresh, first_block, chained):
    refs = list(refs)
    x_ref = refs.pop(0)
    meta_ref = refs.pop(0) if first_block else None
    gcb_ref = refs.pop(0)
    if fresh:
        npre, wpb, bpb, wgm, bgm, wgate, bgate = (refs.pop(0) for _ in range(7))
    else:
        pb_ref, pgm_ref, pg_ref = (refs.pop(0) for _ in range(3))
        qb_in, c_in, n_in, m_in = (refs.pop(0) for _ in range(N_MSTATE))
    wqk, bqk, fb, mln, wmo, wo, npost = (refs.pop(0) for _ in range(7))
    if chained:
        del refs[:N_MSTATE]
    y_ref, qb_out, c_out, n_out, m_out = (refs.pop(0) for _ in range(1 + N_MSTATE))
    qkc, qkh, qks, cw = (refs.pop(0) for _ in range(4))

    c = pl.program_id(1)

    @pl.when(c == 0)
    def _init():
        qkh[...] = jnp.zeros(qkh.shape, F32)
        if fresh:
            c_out[...] = jnp.zeros(c_out.shape, F32)
            n_out[...] = jnp.zeros(n_out.shape, F32)
            m_out[...] = jnp.zeros(m_out.shape, F32)
            cw[...] = jnp.zeros(cw.shape, U32)
        else:
            qkh[QK_BASE:QK_HIST, :] = qb_in[0]
            c_out[...] = c_in[...]
            n_out[...] = n_in[...]
            m_out[...] = m_in[...]
            for h_i in range(ML_HEADS):
                cw[h_i] = _words(c_in[0, h_i].astype(BF16))

    def x_block():
        x = x_ref[0]
        if first_block:
            x0 = jnp.concatenate([jnp.zeros((pad, D_MODEL), F32), meta_ref[...]], axis=0)
            x = jnp.where(c == 0, x0, x)
        return x

    if fresh:
        h = _rms(x_block(), npre[...]).astype(BF16)

        def proj(off, width):
            return _wdot(h, wpb[:, off:off + width]) + bpb[:, off:off + width]

        g = _wdot(h, wgate[...]) + bgate[...]
    else:
        def proj(off, width):
            return pb_ref[0, :, off:off + width]

        g = pg_ref[0]

    row_id = lax.broadcasted_iota(jnp.int32, (L, 1), 0) + c * L
    valid = row_id >= pad

    def qk_chunk(it):
        cs = slice(it * QK_CHUNK, (it + 1) * QK_CHUNK)
        qk_pre = proj(CB_Q + it * QK_CHUNK, QK_CHUNK)
        if pad:
            qk_pre = jnp.where(valid, qk_pre, 0.0)
        qkc[0:QK_HIST, :] = qkh[:, cs]
        qkc[QK_HIST:QK_HIST + L, :] = qk_pre
        acc = qk_pre * wqk[QK_CONV_K - 1:QK_CONV_K, cs] + bqk[:, cs]
        for j in range(QK_CONV_K - 1):
            acc = acc + qkc[QK_BASE + j:QK_BASE + j + L, :] * wqk[j:j + 1, cs]
        qkh[:, cs] = qkc[L:L + QK_HIST, :]
        acc = _silu(acc)
        if it * QK_CHUNK >= D_ML:
            acc = acc * (ML_HEAD_DIM ** -0.5)
        qks[:, cs] = _words(acc.astype(BF16))

    for it in range(2 * D_ML // QK_CHUNK):
        qk_chunk(it)
    qb_out[0] = qkh[QK_BASE:QK_HIST, :]

    logf = jax.nn.log_sigmoid(g + fb[...])
    if pad:
        logf = jnp.where(valid, logf, 0.0)
    b_al = pltpu.roll(_cumsum_rows(logf, L), GATE_W - ML_HEADS, axis=1)
    m_prev = m_out[0]
    inter = b_al + m_prev
    d_src = g - b_al
    d_src_t = d_src.T
    tt = lax.broadcasted_iota(jnp.int32, (L, L), 0)
    ss = lax.broadcasted_iota(jnp.int32, (L, L), 1)
    mask = ss <= tt
    if pad:
        mask = mask & ((ss + c * L) >= pad)

    hms, updates = [], []
    for h_i in range(ML_HEADS):
        hs = slice(h_i * ML_HEAD_DIM, (h_i + 1) * ML_HEAD_DIM)
        v = proj(CB_V + h_i * ML_HEAD_DIM, ML_HEAD_DIM)
        og = proj(CB_O + h_i * ML_HEAD_DIM, ML_HEAD_DIM)
        zm = proj(CB_ZM + h_i * ML_HEAD_DIM, ML_HEAD_DIM)
        q = _halves(qks[:, hs])
        k = _halves(qks[:, D_ML + h_i * ML_HEAD_DIM:D_ML + (h_i + 1) * ML_HEAD_DIM])
        dmat = jnp.where(mask, b_al[:, h_i:h_i + 1] + d_src_t[h_i:h_i + 1, :], -jnp.inf)
        inter_h = inter[:, h_i:h_i + 1]
        m_t = jnp.maximum(inter_h, jnp.max(dmat, axis=-1, keepdims=True))
        w_intra = jnp.exp(dmat - m_t)
        w_inter = jnp.exp(inter_h - m_t)
        s_mat = lax.dot_general(q, k, (((1,), (1,)), ((), ())), preferred_element_type=F32) * w_intra
        n_old = n_out[0, h_i:h_i + 1, :]
        qc = lax.dot_general(q, _halves(cw[h_i]), (((1,), (1,)), ((), ())), preferred_element_type=F32)
        num = jnp.dot(s_mat.astype(BF16), v.astype(BF16), preferred_element_type=F32) + w_inter * qc
        qn = jnp.sum(q.astype(F32) * n_old, axis=-1, keepdims=True)
        den = jnp.sum(s_mat, axis=-1, keepdims=True) + w_inter * qn
        hh = num * (1.0 / jnp.maximum(jnp.abs(den), jnp.exp(-m_t)))
        hh = hh * lax.rsqrt(jnp.mean(hh * hh, axis=-1, keepdims=True) + EPS) * mln[:, hs]
        hm = hh * _sigmoid(og) * _silu(zm)
        hms.append(hm.astype(BF16))

        m_new = m_t[L - 1:L, :]
        decay = jnp.exp(b_al[L - 1:L, h_i:h_i + 1] + m_prev[:, h_i:h_i + 1] - m_new)
        w_s = jnp.exp(d_src[:, h_i:h_i + 1] + (b_al[L - 1:L, h_i:h_i + 1] - m_new))
        if pad:
            w_s = jnp.where(valid, w_s, 0.0)
        vw = (w_s * v).astype(BF16)
        n_new = decay * n_old + jnp.sum(w_s * k.astype(F32), axis=0, keepdims=True)
        updates.append((m_new, decay, vw, k, n_new))

    if fresh:
        gm = _wdot(h, wgm[...]) + bgm[...]
    else:
        gm = pgm_ref[0]
    br_m = _wdot(jnp.concatenate(hms, axis=1), wmo[...])
    ym = gcb_ref[0] + _sigmoid(gm) * br_m
    out = _wdot(ym.astype(BF16), wo[...])
    y_ref[0] = x_block() + _rms(out, npost[...])

    lane = lax.broadcasted_iota(jnp.int32, (1, GATE_W), 1)
    m_row = jnp.zeros((1, GATE_W), F32)
    for h_i, (m_new, decay, vw, k, n_new) in enumerate(updates):
        c_new = decay * c_out[0, h_i] + lax.dot_general(
            vw, k, (((0,), (0,)), ((), ())), preferred_element_type=F32)
        c_out[0, h_i] = c_new
        cw[h_i] = _words(c_new.astype(BF16))
        n_out[0, h_i:h_i + 1, :] = n_new
        m_row = jnp.where(lane == h_i, m_new, m_row)
    m_out[0] = m_row


def _mstate_shapes(depth, S):
    return [
        (depth, S, QK_CONV_K - 1, 2 * D_ML),
        (depth, S, ML_HEADS, ML_HEAD_DIM, ML_HEAD_DIM),
        (depth, S, ML_HEADS, ML_HEAD_DIM),
        (depth, S, 1, GATE_W),
    ]


def _mlstm_branch(x, gcb, lw, layer, depth, L, *, meta=None, proj=None, state=None, prev_out=None,
                  pad=0, drop_first=False):
    fresh = proj is None
    first_block = meta is not None
    chained = prev_out is not None
    S, T = gcb.shape[0], gcb.shape[1]
    shapes = _mstate_shapes(depth, S)
    shift = lambda c: jnp.maximum(c - 1, 0)
    blk = lambda s, c: (s, c, 0)
    full2 = lambda s, c: (0, 0)
    const = dict(pipeline_mode=pl.Buffered(1))

    def state_spec(shape, **kw):
        nd = len(shape) - 2
        return pl.BlockSpec((None, 1) + shape[2:], lambda s, c: (layer, s) + (0,) * nd, **kw)

    in_specs = [pl.BlockSpec((1, L, D_MODEL), (lambda s, c: (s, shift(c), 0)) if first_block else blk)]
    args = [x]
    if first_block:
        in_specs.append(pl.BlockSpec(meta.shape, full2, **const))
        args.append(meta)
    in_specs.append(pl.BlockSpec((1, L, D_MODEL), blk))
    args.append(gcb)
    if fresh:
        ws = [lw["norm_pre"], lw["w_pb"], lw["b_pb"], lw["w_gm"], lw["b_gm"], lw["w_gate"], lw["b_gate"]]
    else:
        in_specs += [pl.BlockSpec((1, L, D_PB), blk), pl.BlockSpec((1, L, D_MODEL), blk),
                     pl.BlockSpec((1, L, GATE_W), blk)]
        args += list(proj)
        in_specs += [state_spec(sh) for sh in shapes]
        args += list(state)
        ws = []
    ws += [lw["w_qkc"], lw["b_qkc"], lw["f_bias"], lw["ml_norm"], lw["w_ml_out"], lw["w_out"],
           lw["norm_post"]]
    in_specs += [pl.BlockSpec(w.shape, full2, **const) for w in ws]
    args += ws
    aliases = {}
    if chained:
        for k, a in enumerate(prev_out):
            aliases[len(args)] = 1 + k
            in_specs.append(pl.BlockSpec(memory_space=pl.ANY))
            args.append(a)

    t_out = T - L if drop_first else T
    out_shape = [jax.ShapeDtypeStruct((S, t_out, D_MODEL), F32)] + \
        [jax.ShapeDtypeStruct(sh, F32) for sh in shapes]
    out_specs = [pl.BlockSpec((1, L, D_MODEL), (lambda s, c: (s, shift(c), 0)) if drop_first else blk)] + \
        [state_spec(sh) for sh in shapes]
    scratch = [
        pltpu.VMEM((L + QK_HIST, QK_CHUNK), F32),
        pltpu.VMEM((QK_HIST, 2 * D_ML), F32),
        pltpu.VMEM((L // 2, 2 * D_ML), U32),
        pltpu.VMEM((ML_HEADS, ML_HEAD_DIM // 2, ML_HEAD_DIM), U32),
    ]
    outs = pl.pallas_call(
        functools.partial(_mlstm_kernel, L=L, pad=pad, fresh=fresh, first_block=first_block,
                          chained=chained),
        grid=(S, T // L),
        in_specs=in_specs,
        out_specs=out_specs,
        out_shape=out_shape,
        scratch_shapes=scratch,
        input_output_aliases=aliases,
        compiler_params=_params(2),
        name="mlstm_fresh" if fresh else "mlstm_carry",
    )(*args)
    return outs[0], outs[1:]


L_PROMPT = 256
TM_SAMPLE = 1024


def _layer_weights(l, norm_pre, norm_post, w_in, b_in, w_dw, b_dw, ln_g, ln_b, w_qk_conv, b_qk_conv,
                   f_bias, ml_norm, w_conv_out, w_ml_out, w_out):
    b = b_in[l]
    tail = w_in[l, :, LO_IF:]
    w_gate = jnp.pad(tail[:, :N_IF], ((0, 0), (0, GATE_W - N_IF)))[None]
    w_gc = tail[:, N_IF:N_IF + D_MODEL][None]
    w_gm = tail[:, N_IF + D_MODEL:][None]
    row = lambda a: a[None, :]
    return dict(
        norm_pre=row(norm_pre[l]), norm_post=row(norm_post[l]),
        w_pa=_pack(w_in, l, 0, D_PA), b_pa=row(b[:D_PA]),
        w_pb=_pack(w_in, l, D_PA, D_PB), b_pb=row(b[D_PA:LO_IF]),
        w_gate=_pack(w_gate, 0), b_gate=row(jnp.pad(b[LO_IF:LO_GC], (0, GATE_W - N_IF))),
        w_gc=_pack(w_gc, 0), b_gc=row(b[LO_GC:LO_GM]),
        w_gm=_pack(w_gm, 0), b_gm=row(b[LO_GM:]),
        w_dw=w_dw[l], b_dw=row(b_dw[l]), ln_g=row(ln_g[l]), ln_b=row(ln_b[l]),
        w_qkc=w_qk_conv[l], b_qkc=row(b_qk_conv[l]),
        f_bias=row(jnp.pad(f_bias[l], (ML_HEADS, GATE_W - N_IF))), ml_norm=row(ml_norm[l]),
        w_conv_out=_pack(w_conv_out, l), w_ml_out=_pack(w_ml_out, l), w_out=_pack(w_out, l))


def kernel(x_prompt, x_sample, state_conv, state_qk_conv, state_C, state_n, state_m, meta_tokens,
           norm_pre, norm_post, w_in, b_in, w_dw, b_dw, ln_g, ln_b, w_qk_conv, b_qk_conv, f_bias,
           ml_norm, w_conv_out, w_ml_out, w_out):
    bp, seq, _ = x_prompt.shape
    bs, dec_seq, _ = x_sample.shape
    depth = w_in.shape[0]
    L = L_PROMPT
    pad = L - N_META
    assert seq % L == 0 and pad % SUBLANES == 0 and dec_seq % (2 * SUBLANES) == 0

    meta = meta_tokens.astype(x_prompt.dtype)
    m_pad = jnp.pad(state_m, ((0, 0), (0, 0), (0, GATE_W - ML_HEADS)))[:, :, None, :]
    mstate_in = (state_qk_conv, state_C, state_n, m_pad)

    xp, xs = x_prompt, x_sample
    cb_p = cb_s = st_p = st_s = None
    for l in range(depth):
        lw = _layer_weights(l, norm_pre, norm_post, w_in, b_in, w_dw, b_dw, ln_g, ln_b, w_qk_conv,
                            b_qk_conv, f_bias, ml_norm, w_conv_out, w_ml_out, w_out)
        first = meta if l == 0 else None
        gcb, cb_p = _conv_branch(xp, lw, l, depth, L, meta=first, prev_out=cb_p, pad=pad)
        xp, st_p = _mlstm_branch(xp, gcb, lw, l, depth, L, meta=first, prev_out=st_p, pad=pad,
                                 drop_first=(l == depth - 1))

        xs2 = xs.reshape(bs * dec_seq, D_MODEL)
        tok = lambda a: a.reshape(bs, dec_seq, a.shape[-1])
        (pa,) = _inproj(xs2, lw["norm_pre"], lw["w_pa"], lw["b_pa"], TM_SAMPLE)
        (pgc,) = _inproj(xs2, lw["norm_pre"], lw["w_gc"], lw["b_gc"], TM_SAMPLE)
        pb, pg = _inproj(xs2, lw["norm_pre"], lw["w_pb"], lw["b_pb"], TM_SAMPLE, lw["w_gate"], lw["b_gate"])
        (pgm,) = _inproj(xs2, lw["norm_pre"], lw["w_gm"], lw["b_gm"], TM_SAMPLE)
        gcb, cb_s = _conv_branch(None, lw, l, depth, dec_seq, proj=(tok(pa), tok(pgc)),
                                 state=state_conv, prev_out=cb_s)
        xs, st_s = _mlstm_branch(xs, gcb, lw, l, depth, dec_seq, proj=(tok(pb), tok(pgm), tok(pg)),
                                 state=mstate_in, prev_out=st_s)

    fin = lambda cb, st: (cb,) + tuple(st[:3]) + (st[3][:, :, 0, :ML_HEADS],)
    return (xp, xs) + fin(cb_p, st_p) + fin(cb_s, st_s)
```

```python
import functools

import jax
import jax.numpy as jnp
from jax import lax
from jax.experimental import pallas as pl
from jax.experimental.pallas import tpu as pltpu

F32 = jnp.float32
BF16 = jnp.bfloat16
U32 = jnp.uint32

D_MODEL = 1024
D_CONV = D_MODEL
CONV_K = 31
D_ML = 2 * D_MODEL
ML_HEADS = 4
ML_HEAD_DIM = D_ML // ML_HEADS
QK_CONV_K = 4
N_META = 16
EPS = 1e-6

LANES = 128
SUBLANES = 8
GATE_W = LANES
N_IF = 2 * ML_HEADS
CONV_HIST = 32
CONV_BASE = CONV_HIST - (CONV_K - 1)
CONV_SPAN = (CONV_BASE + CONV_K - 2) // SUBLANES * SUBLANES
QK_HIST = SUBLANES
QK_BASE = QK_HIST - (QK_CONV_K - 1)
QK_CHUNK = ML_HEAD_DIM
CONV_ROWS = 128
GLU_CHUNK = 2 * LANES
TN = 1024
VMEM_LIMIT = 62 * 1024 * 1024

D_PA = 3 * D_CONV
D_PB = 5 * D_ML
LO_IF = D_PA + D_PB
LO_GC = LO_IF + N_IF
LO_GM = LO_GC + D_MODEL
CA_A, CA_GL, CA_ZC = 0, D_CONV, 2 * D_CONV
CB_Q, CB_K, CB_V, CB_O, CB_ZM = 0, D_ML, 2 * D_ML, 3 * D_ML, 4 * D_ML


def _sigmoid(x):
    return jax.nn.sigmoid(x)


def _silu(x):
    return x * jax.nn.sigmoid(x)


def _rms(x, g):
    return x * lax.rsqrt(jnp.mean(x * x, axis=-1, keepdims=True) + EPS) * g


def _words(x_bf16):
    return pltpu.bitcast(x_bf16, U32)


def _halves(x_u32):
    return pltpu.bitcast(x_u32, BF16)


def _wdot(x_bf16, w_words):
    return jnp.dot(x_bf16, _halves(w_words), preferred_element_type=F32)


def _wdot_t(x_bf16, wt_words):
    return lax.dot_general(x_bf16, _halves(wt_words), (((1,), (1,)), ((), ())),
                           preferred_element_type=F32)


def _params(n_grid):
    return pltpu.CompilerParams(dimension_semantics=("arbitrary",) * n_grid,
                                vmem_limit_bytes=VMEM_LIMIT)


def _pack_kernel(w_ref, o_ref):
    o_ref[...] = _words(w_ref[...].astype(BF16))


def _pack(w3d, layer, col_lo=0, n_col=None):
    depth, k, n = w3d.shape
    n_col = n - col_lo if n_col is None else n_col
    tn = min(n_col, TN)
    assert col_lo % tn == 0 and n_col % tn == 0
    j0 = col_lo // tn
    return pl.pallas_call(
        _pack_kernel,
        grid=(n_col // tn,),
        in_specs=[pl.BlockSpec((k, tn), lambda j: (layer, j + j0))],
        out_specs=pl.BlockSpec((k // 2, tn), lambda j: (0, j)),
        out_shape=jax.ShapeDtypeStruct((k // 2, n_col), U32),
        compiler_params=_params(1),
        name="pack_weight",
    )(w3d.reshape(depth * k, n))


def _pack_t(wt3d, layer, row_lo=0, n_row=None):
    _, n, k = wt3d.shape
    n_row = n - row_lo if n_row is None else n_row
    tn = min(n_row, TN)
    assert row_lo % tn == 0 and n_row % tn == 0
    j0 = row_lo // tn
    return pl.pallas_call(
        _pack_kernel,
        grid=(n_row // tn,),
        in_specs=[pl.BlockSpec((None, tn, k), lambda j: (layer, j + j0, 0))],
        out_specs=pl.BlockSpec((tn // 2, k), lambda j: (j, 0)),
        out_shape=jax.ShapeDtypeStruct((n_row // 2, k), U32),
        compiler_params=_params(1),
        name="pack_weight_t",
    )(wt3d)


def _inproj_kernel(*refs, gated):
    if gated:
        x_ref, g_ref, w_ref, b_ref, wg_ref, bg_ref, o_ref, og_ref, h_ref = refs
    else:
        x_ref, g_ref, w_ref, b_ref, o_ref, h_ref = refs

    @pl.when(pl.program_id(1) == 0)
    def _():
        hb = _rms(x_ref[...], g_ref[...]).astype(BF16)
        h_ref[...] = _words(hb)
        if gated:
            og_ref[...] = _wdot_t(hb, wg_ref[...]) + bg_ref[...]

    o_ref[...] = _wdot_t(_halves(h_ref[...]), w_ref[...]) + b_ref[...]


def _inproj(x2d, g, wt_words, b, tm, w_gate=None, b_gate=None):
    n_tok, n_col = x2d.shape[0], 2 * wt_words.shape[0]
    gated = w_gate is not None
    in_specs = [
        pl.BlockSpec((tm, D_MODEL), lambda i, j: (i, 0)),
        pl.BlockSpec((1, D_MODEL), lambda i, j: (0, 0)),
        pl.BlockSpec((TN // 2, D_MODEL), lambda i, j: (j, 0)),
        pl.BlockSpec((1, TN), lambda i, j: (0, j)),
    ]
    out_specs = [pl.BlockSpec((tm, TN), lambda i, j: (i, j))]
    out_shape = [jax.ShapeDtypeStruct((n_tok, n_col), F32)]
    args = [x2d, g, wt_words, b]
    if gated:
        in_specs += [pl.BlockSpec((GATE_W // 2, D_MODEL), lambda i, j: (0, 0)),
                     pl.BlockSpec((1, GATE_W), lambda i, j: (0, 0))]
        out_specs.append(pl.BlockSpec((tm, GATE_W), lambda i, j: (i, 0)))
        out_shape.append(jax.ShapeDtypeStruct((n_tok, GATE_W), F32))
        args += [w_gate, b_gate]
    return pl.pallas_call(
        functools.partial(_inproj_kernel, gated=gated),
        grid=(n_tok // tm, n_col // TN),
        in_specs=in_specs, out_specs=out_specs, out_shape=out_shape,
        scratch_shapes=[pltpu.VMEM((tm // 2, D_MODEL), U32)],
        compiler_params=_params(2),
        name="inproj",
    )(*args)


def _cumsum_rows(x, n_rows):
    row = lax.broadcasted_iota(jnp.int32, x.shape, 0)
    k = 1
    while k < n_rows:
        x = x + jnp.where(row >= k, pltpu.roll(x, k, axis=0), 0.0)
        k *= 2
    return x


def _conv_kernel(*refs, L, pad, fresh, first_block, chained):
    refs = list(refs)
    x_ref = refs.pop(0) if fresh else None
    meta_ref = refs.pop(0) if first_block else None
    if fresh:
        npre, wpa, bpa, wgc, bgc = (refs.pop(0) for _ in range(5))
    else:
        pa_ref, pgc_ref, cb_in = (refs.pop(0) for _ in range(3))
    wdw, bdw, lng, lnb, wco = (refs.pop(0) for _ in range(5))
    if chained:
        refs.pop(0)
    gcb_ref, cb_out = refs.pop(0), refs.pop(0)
    ubuf, ush, cvo = (refs.pop(0) for _ in range(3))

    c = pl.program_id(1)

    @pl.when(c == 0)
    def _init():
        ubuf[0:CONV_HIST, :] = jnp.zeros((CONV_HIST, D_CONV), F32)
        if not fresh:
            ubuf[CONV_BASE:CONV_HIST, :] = cb_in[0]

    if fresh:
        x = x_ref[0]
        if first_block:
            x0 = jnp.concatenate([jnp.zeros((pad, D_MODEL), F32), meta_ref[...]], axis=0)
            x = jnp.where(c == 0, x0, x)
        h = _rms(x, npre[...]).astype(BF16)

        def proj_a(off, width):
            return _wdot_t(h, wpa[off // 2:(off + width) // 2, :]) + bpa[:, off:off + width]

        def proj_gc(off, width):
            return _wdot_t(h, wgc[off // 2:(off + width) // 2, :]) + bgc[:, off:off + width]
    else:
        def proj_a(off, width):
            return pa_ref[0, :, off:off + width]

        def proj_gc(off, width):
            return pgc_ref[0, :, off:off + width]

    if pad:
        row_id = lax.broadcasted_iota(jnp.int32, (L, 1), 0) + c * L
        valid = row_id >= pad

    rc = min(L, CONV_ROWS)
    n_chunk = D_CONV // GLU_CHUNK
    for ck in range(n_chunk):
        co = ck * GLU_CHUNK
        u = proj_a(CA_A + co, GLU_CHUNK) * _sigmoid(proj_a(CA_GL + co, GLU_CHUNK))
        if pad:
            u = jnp.where(valid, u, 0.0)
        ubuf[CONV_HIST:CONV_HIST + L, co:co + GLU_CHUNK] = u
        if ck == n_chunk - 2:
            zc = proj_a(CA_ZC, D_CONV)
        if ck == n_chunk - 1:
            gc = proj_gc(0, D_MODEL)
        for it in range(co // LANES, (co + GLU_CHUNK) // LANES):
            ls = slice(it * LANES, (it + 1) * LANES)
            sb = (it % 2) * (SUBLANES - 1)
            for sh in range(1, SUBLANES):
                ush[sb + sh - 1, :, :] = ubuf[sh:sh + L + CONV_SPAN, ls]
            for r0 in range(0, L, rc):
                acc = jnp.zeros((rc, LANES), F32) + bdw[:, ls]
                for j in range(CONV_K):
                    a8, sh = divmod(CONV_BASE + j, SUBLANES)
                    lo = a8 * SUBLANES + r0
                    src = ubuf[lo:lo + rc, ls] if sh == 0 else ush[sb + sh - 1, lo:lo + rc, :]
                    acc = acc + src * wdw[j:j + 1, ls]
                cvo[r0:r0 + rc, ls] = acc
    cb_out[0] = ubuf[L + CONV_BASE:L + CONV_HIST, :]
    ubuf[0:CONV_HIST, :] = ubuf[L:L + CONV_HIST, :]

    cv = cvo[...]
    mu = jnp.mean(cv, axis=-1, keepdims=True)
    xc = cv - mu
    ln = xc * lax.rsqrt(jnp.mean(xc * xc, axis=-1, keepdims=True) + EPS) * lng[...] + lnb[...]
    ua = _silu(ln) * _silu(zc)
    br_c = _wdot(ua.astype(BF16), wco[...])
    gcb_ref[0] = _sigmoid(gc) * br_c


def _conv_branch(x, lw, layer, depth, L, *, meta=None, proj=None, state=None, prev_out=None, pad=0):
    fresh = proj is None
    first_block = meta is not None
    chained = prev_out is not None
    src = x if fresh else proj[0]
    S = src.shape[0]
    T = src.shape[1] + (L if first_block else 0)
    shift = lambda c: jnp.maximum(c - 1, 0)
    blk = lambda s, c: (s, c, 0)
    full2 = lambda s, c: (0, 0)
    const = dict(pipeline_mode=pl.Buffered(1))
    st_shape = (depth, S, CONV_K - 1, D_CONV)
    st_spec = pl.BlockSpec((None, 1) + st_shape[2:], lambda s, c: (layer, s, 0, 0))

    in_specs, args = [], []
    if fresh:
        in_specs.append(pl.BlockSpec((1, L, D_MODEL), (lambda s, c: (s, shift(c), 0)) if first_block
                                     else blk))
        args.append(x)
        if first_block:
            in_specs.append(pl.BlockSpec(meta.shape, full2, **const))
            args.append(meta)
        ws = [lw["norm_pre"], lw["w_pa"], lw["b_pa"], lw["w_gc"], lw["b_gc"]]
    else:
        in_specs += [pl.BlockSpec((1, L, D_PA), blk), pl.BlockSpec((1, L, D_MODEL), blk), st_spec]
        args += [proj[0], proj[1], state]
        ws = []
    ws += [lw["w_dw"], lw["b_dw"], lw["ln_g"], lw["ln_b"], lw["w_conv_out"]]
    in_specs += [pl.BlockSpec(w.shape, full2, **const) for w in ws]
    args += ws
    aliases = {}
    if chained:
        aliases[len(args)] = 1
        in_specs.append(pl.BlockSpec(memory_space=pl.ANY))
        args.append(prev_out)

    scratch = [
        pltpu.VMEM((L + CONV_HIST, D_CONV), F32),
        pltpu.VMEM((2 * (SUBLANES - 1), L + CONV_SPAN, LANES), F32),
        pltpu.VMEM((L, D_CONV), F32),
    ]
    return pl.pallas_call(
        functools.partial(_conv_kernel, L=L, pad=pad, fresh=fresh, first_block=first_block,
                          chained=chained),
        grid=(S, T // L),
        in_specs=in_specs,
        out_specs=[pl.BlockSpec((1, L, D_MODEL), blk), st_spec],
        out_shape=[jax.ShapeDtypeStruct((S, T, D_MODEL), F32), jax.ShapeDtypeStruct(st_shape, F32)],
        scratch_shapes=scratch,
        input_output_aliases=aliases,
        compiler_params=_params(2),
        name="conv_fresh" if fresh else "conv_carry",
    )(*args)


N_MSTATE = 4


def _mlstm_kernel(*refs, L, pad, fresh, first_block, chained):
    refs = list(refs)
    x_ref = refs.pop(0)
    meta_ref = refs.pop(0) if first_block else None
    gcb_ref = refs.pop(0)
    if fresh:
        npre, wpb, bpb, wgm, bgm, wgate, bgate = (refs.pop(0) for _ in range(7))
    else:
        pb_ref, pgm_ref, pg_ref = (refs.pop(0) for _ in range(3))
        qb_in, c_in, n_in, m_in = (refs.pop(0) for _ in range(N_MSTATE))
    wqk, bqk, fb, mln, wmo, wo, npost = (refs.pop(0) for _ in range(7))
    if chained:
        del refs[:N_MSTATE]
    y_ref, qb_out, c_out, n_out, m_out = (refs.pop(0) for _ in range(1 + N_MSTATE))
    qkc, qkh, qks, cw = (refs.pop(0) for _ in range(4))

    c = pl.program_id(1)

    @pl.when(c == 0)
    def _init():
        qkh[...] = jnp.zeros(qkh.shape, F32)
        if fresh:
            c_out[...] = jnp.zeros(c_out.shape, F32)
            n_out[...] = jnp.zeros(n_out.shape, F32)
            m_out[...] = jnp.zeros(m_out.shape, F32)
            cw[...] = jnp.zeros(cw.shape, U32)
        else:
            qkh[QK_BASE:QK_HIST, :] = qb_in[0]
            c_out[...] = c_in[...]
            n_out[...] = n_in[...]
            m_out[...] = m_in[...]
            for h_i in range(ML_HEADS):
                cw[h_i] = _words(c_in[0, h_i].astype(BF16))

    def x_block():
        x = x_ref[0]
        if first_block:
            x0 = jnp.concatenate([jnp.zeros((pad, D_MODEL), F32), meta_ref[...]], axis=0)
            x = jnp.where(c == 0, x0, x)
        return x

    if fresh:
        h = _rms(x_block(), npre[...]).astype(BF16)

        def proj(off, width):
            return _wdot_t(h, wpb[off // 2:(off + width) // 2, :]) + bpb[:, off:off + width]

        g = _wdot_t(h, wgate[...]) + bgate[...]
    else:
        def proj(off, width):
            return pb_ref[0, :, off:off + width]

        g = pg_ref[0]

    row_id = lax.broadcasted_iota(jnp.int32, (L, 1), 0) + c * L
    valid = row_id >= pad

    def qk_chunk(it):
        cs = slice(it * QK_CHUNK, (it + 1) * QK_CHUNK)
        qk_pre = proj(CB_Q + it * QK_CHUNK, QK_CHUNK)
        if pad:
            qk_pre = jnp.where(valid, qk_pre, 0.0)
        qkc[0:QK_HIST, :] = qkh[:, cs]
        qkc[QK_HIST:QK_HIST + L, :] = qk_pre
        acc = qk_pre * wqk[QK_CONV_K - 1:QK_CONV_K, cs] + bqk[:, cs]
        for j in range(QK_CONV_K - 1):
            acc = acc + qkc[QK_BASE + j:QK_BASE + j + L, :] * wqk[j:j + 1, cs]
        qkh[:, cs] = qkc[L:L + QK_HIST, :]
        acc = _silu(acc)
        if it * QK_CHUNK >= D_ML:
            acc = acc * (ML_HEAD_DIM ** -0.5)
        qks[:, cs] = _words(acc.astype(BF16))

    for it in range(2 * D_ML // QK_CHUNK):
        qk_chunk(it)
    qb_out[0] = qkh[QK_BASE:QK_HIST, :]

    logf = jax.nn.log_sigmoid(g + fb[...])
    if pad:
        logf = jnp.where(valid, logf, 0.0)
    b_al = pltpu.roll(_cumsum_rows(logf, L), GATE_W - ML_HEADS, axis=1)
    m_prev = m_out[0]
    inter = b_al + m_prev
    d_src = g - b_al
    d_src_t = d_src.T
    tt = lax.broadcasted_iota(jnp.int32, (L, L), 0)
    ss = lax.broadcasted_iota(jnp.int32, (L, L), 1)
    mask = ss <= tt
    if pad:
        mask = mask & ((ss + c * L) >= pad)

    hms, updates = [], []
    for h_i in range(ML_HEADS):
        hs = slice(h_i * ML_HEAD_DIM, (h_i + 1) * ML_HEAD_DIM)
        v = proj(CB_V + h_i * ML_HEAD_DIM, ML_HEAD_DIM)
        og = proj(CB_O + h_i * ML_HEAD_DIM, ML_HEAD_DIM)
        zm = proj(CB_ZM + h_i * ML_HEAD_DIM, ML_HEAD_DIM)
        q = _halves(qks[:, hs])
        k = _halves(qks[:, D_ML + h_i * ML_HEAD_DIM:D_ML + (h_i + 1) * ML_HEAD_DIM])
        dmat = jnp.where(mask, b_al[:, h_i:h_i + 1] + d_src_t[h_i:h_i + 1, :], -jnp.inf)
        inter_h = inter[:, h_i:h_i + 1]
        m_t = jnp.maximum(inter_h, jnp.max(dmat, axis=-1, keepdims=True))
        w_intra = jnp.exp(dmat - m_t)
        w_inter = jnp.exp(inter_h - m_t)
        s_mat = lax.dot_general(q, k, (((1,), (1,)), ((), ())), preferred_element_type=F32) * w_intra
        n_old = n_out[0, h_i:h_i + 1, :]
        qc = lax.dot_general(q, _halves(cw[h_i]), (((1,), (1,)), ((), ())), preferred_element_type=F32)
        num = jnp.dot(s_mat.astype(BF16), v.astype(BF16), preferred_element_type=F32) + w_inter * qc
        qn = jnp.sum(q.astype(F32) * n_old, axis=-1, keepdims=True)
        den = jnp.sum(s_mat, axis=-1, keepdims=True) + w_inter * qn
        hh = num * (1.0 / jnp.maximum(jnp.abs(den), jnp.exp(-m_t)))
        hh = hh * lax.rsqrt(jnp.mean(hh * hh, axis=-1, keepdims=True) + EPS) * mln[:, hs]
        hm = hh * _sigmoid(og) * _silu(zm)
        hms.append(hm.astype(BF16))

        m_new = m_t[L - 1:L, :]
        decay = jnp.exp(b_al[L - 1:L, h_i:h_i + 1] + m_prev[:, h_i:h_i + 1] - m_new)
        w_s = jnp.exp(d_src[:, h_i:h_i + 1] + (b_al[L - 1:L, h_i:h_i + 1] - m_new))
        if pad:
            w_s = jnp.where(valid, w_s, 0.0)
        vw = (w_s * v).astype(BF16)
        n_new = decay * n_old + jnp.sum(w_s * k.astype(F32), axis=0, keepdims=True)
        updates.append((m_new, decay, vw, k, n_new))

    if fresh:
        gm = _wdot_t(h, wgm[...]) + bgm[...]
    else:
        gm = pgm_ref[0]
    br_m = _wdot(jnp.concatenate(hms, axis=1), wmo[...])
    ym = gcb_ref[0] + _sigmoid(gm) * br_m
    out = _wdot(ym.astype(BF16), wo[...])
    y_ref[0] = x_block() + _rms(out, npost[...])

    lane = lax.broadcasted_iota(jnp.int32, (1, GATE_W), 1)
    m_row = jnp.zeros((1, GATE_W), F32)
    for h_i, (m_new, decay, vw, k, n_new) in enumerate(updates):
        c_new = decay * c_out[0, h_i] + lax.dot_general(
            vw, k, (((0,), (0,)), ((), ())), preferred_element_type=F32)
        c_out[0, h_i] = c_new
        cw[h_i] = _words(c_new.astype(BF16))
        n_out[0, h_i:h_i + 1, :] = n_new
        m_row = jnp.where(lane == h_i, m_new, m_row)
    m_out[0] = m_row


def _mstate_shapes(depth, S):
    return [
        (depth, S, QK_CONV_K - 1, 2 * D_ML),
        (depth, S, ML_HEADS, ML_HEAD_DIM, ML_HEAD_DIM),
        (depth, S, ML_HEADS, ML_HEAD_DIM),
        (depth, S, 1, GATE_W),
    ]


def _mlstm_branch(x, gcb, lw, layer, depth, L, *, meta=None, proj=None, state=None, prev_out=None,
                  pad=0, drop_first=False):
    fresh = proj is None
    first_block = meta is not None
    chained = prev_out is not None
    S, T = gcb.shape[0], gcb.shape[1]
    shapes = _mstate_shapes(depth, S)
    shift = lambda c: jnp.maximum(c - 1, 0)
    blk = lambda s, c: (s, c, 0)
    full2 = lambda s, c: (0, 0)
    const = dict(pipeline_mode=pl.Buffered(1))

    def state_spec(shape, **kw):
        nd = len(shape) - 2
        return pl.BlockSpec((None, 1) + shape[2:], lambda s, c: (layer, s) + (0,) * nd, **kw)

    in_specs = [pl.BlockSpec((1, L, D_MODEL), (lambda s, c: (s, shift(c), 0)) if first_block else blk)]
    args = [x]
    if first_block:
        in_specs.append(pl.BlockSpec(meta.shape, full2, **const))
        args.append(meta)
    in_specs.append(pl.BlockSpec((1, L, D_MODEL), blk))
    args.append(gcb)
    if fresh:
        ws = [lw["norm_pre"], lw["w_pb"], lw["b_pb"], lw["w_gm"], lw["b_gm"], lw["w_gate"], lw["b_gate"]]
    else:
        in_specs += [pl.BlockSpec((1, L, D_PB), blk), pl.BlockSpec((1, L, D_MODEL), blk),
                     pl.BlockSpec((1, L, GATE_W), blk)]
        args += list(proj)
        in_specs += [state_spec(sh) for sh in shapes]
        args += list(state)
        ws = []
    ws += [lw["w_qkc"], lw["b_qkc"], lw["f_bias"], lw["ml_norm"], lw["w_ml_out"], lw["w_out"],
           lw["norm_post"]]
    in_specs += [pl.BlockSpec(w.shape, full2, **const) for w in ws]
    args += ws
    aliases = {}
    if chained:
        for k, a in enumerate(prev_out):
            aliases[len(args)] = 1 + k
            in_specs.append(pl.BlockSpec(memory_space=pl.ANY))
            args.append(a)

    t_out = T - L if drop_first else T
    out_shape = [jax.ShapeDtypeStruct((S, t_out, D_MODEL), F32)] + \
        [jax.ShapeDtypeStruct(sh, F32) for sh in shapes]
    out_specs = [pl.BlockSpec((1, L, D_MODEL), (lambda s, c: (s, shift(c), 0)) if drop_first else blk)] + \
        [state_spec(sh) for sh in shapes]
    scratch = [
        pltpu.VMEM((L + QK_HIST, QK_CHUNK), F32),
        pltpu.VMEM((QK_HIST, 2 * D_ML), F32),
        pltpu.VMEM((L // 2, 2 * D_ML), U32),
        pltpu.VMEM((ML_HEADS, ML_HEAD_DIM // 2, ML_HEAD_DIM), U32),
    ]
    outs = pl.pallas_call(
        functools.partial(_mlstm_kernel, L=L, pad=pad, fresh=fresh, first_block=first_block,
                          chained=chained),
        grid=(S, T // L),
        in_specs=in_specs,
        out_specs=out_specs,
        out_shape=out_shape,
        scratch_shapes=scratch,
        input_output_aliases=aliases,
        compiler_params=_params(2),
        name="mlstm_fresh" if fresh else "mlstm_carry",
    )(*args)
    return outs[0], outs[1:]


L_PROMPT = 256
TM_SAMPLE = 1024


def _layer_weights(l, norm_pre, norm_post, w_in_t, b_in, w_dw, b_dw, ln_g, ln_b, w_qk_conv, b_qk_conv,
                   f_bias, ml_norm, w_conv_out, w_ml_out, w_out):
    b = b_in[l]
    tail = w_in_t[l, LO_IF:, :]
    w_gate = jnp.pad(tail[:N_IF], ((0, GATE_W - N_IF), (0, 0)))[None]
    w_gc = tail[N_IF:N_IF + D_MODEL][None]
    w_gm = tail[N_IF + D_MODEL:][None]
    row = lambda a: a[None, :]
    return dict(
        norm_pre=row(norm_pre[l]), norm_post=row(norm_post[l]),
        w_pa=_pack_t(w_in_t, l, 0, D_PA), b_pa=row(b[:D_PA]),
        w_pb=_pack_t(w_in_t, l, D_PA, D_PB), b_pb=row(b[D_PA:LO_IF]),
        w_gate=_pack_t(w_gate, 0), b_gate=row(jnp.pad(b[LO_IF:LO_GC], (0, GATE_W - N_IF))),
        w_gc=_pack_t(w_gc, 0), b_gc=row(b[LO_GC:LO_GM]),
        w_gm=_pack_t(w_gm, 0), b_gm=row(b[LO_GM:]),
        w_dw=w_dw[l], b_dw=row(b_dw[l]), ln_g=row(ln_g[l]), ln_b=row(ln_b[l]),
        w_qkc=w_qk_conv[l], b_qkc=row(b_qk_conv[l]),
        f_bias=row(jnp.pad(f_bias[l], (ML_HEADS, GATE_W - N_IF))), ml_norm=row(ml_norm[l]),
        w_conv_out=_pack(w_conv_out, l), w_ml_out=_pack(w_ml_out, l), w_out=_pack(w_out, l))


def kernel(x_prompt, x_sample, state_conv, state_qk_conv, state_C, state_n, state_m, meta_tokens,
           norm_pre, norm_post, w_in, b_in, w_dw, b_dw, ln_g, ln_b, w_qk_conv, b_qk_conv, f_bias,
           ml_norm, w_conv_out, w_ml_out, w_out):
    bp, seq, _ = x_prompt.shape
    bs, dec_seq, _ = x_sample.shape
    depth = w_in.shape[0]
    L = L_PROMPT
    pad = L - N_META
    assert seq % L == 0 and pad % SUBLANES == 0 and dec_seq % (2 * SUBLANES) == 0

    meta = meta_tokens.astype(x_prompt.dtype)
    m_pad = jnp.pad(state_m, ((0, 0), (0, 0), (0, GATE_W - ML_HEADS)))[:, :, None, :]
    w_in_t = jnp.swapaxes(w_in, 1, 2)
    mstate_in = (state_qk_conv, state_C, state_n, m_pad)

    xp, xs = x_prompt, x_sample
    cb_p = cb_s = st_p = st_s = None
    for l in range(depth):
        lw = _layer_weights(l, norm_pre, norm_post, w_in_t, b_in, w_dw, b_dw, ln_g, ln_b, w_qk_conv,
                            b_qk_conv, f_bias, ml_norm, w_conv_out, w_ml_out, w_out)
        first = meta if l == 0 else None
        gcb, cb_p = _conv_branch(xp, lw, l, depth, L, meta=first, prev_out=cb_p, pad=pad)
        xp, st_p = _mlstm_branch(xp, gcb, lw, l, depth, L, meta=first, prev_out=st_p, pad=pad,
                                 drop_first=(l == depth - 1))

        xs2 = xs.reshape(bs * dec_seq, D_MODEL)
        tok = lambda a: a.reshape(bs, dec_seq, a.shape[-1])
        (pa,) = _inproj(xs2, lw["norm_pre"], lw["w_pa"], lw["b_pa"], TM_SAMPLE)
        (pgc,) = _inproj(xs2, lw["norm_pre"], lw["w_gc"], lw["b_gc"], TM_SAMPLE)
        pb, pg = _inproj(xs2, lw["norm_pre"], lw["w_pb"], lw["b_pb"], TM_SAMPLE, lw["w_gate"], lw["b_gate"])
        (pgm,) = _inproj(xs2, lw["norm_pre"], lw["w_gm"], lw["b_gm"], TM_SAMPLE)
        gcb, cb_s = _conv_branch(None, lw, l, depth, dec_seq, proj=(tok(pa), tok(pgc)),
                                 state=state_conv, prev_out=cb_s)
        xs, st_s = _mlstm_branch(xs, gcb, lw, l, depth, dec_seq, proj=(tok(pb), tok(pgm), tok(pg)),
                                 state=mstate_in, prev_out=st_s)

    fin = lambda cb, st: (cb,) + tuple(st[:3]) + (st[3][:, :, 0, :ML_HEADS],)
    return (xp, xs) + fin(cb_p, st_p) + fin(cb_s, st_s)
```

```python
import functools

import jax
import jax.numpy as jnp
from jax import lax
from jax.experimental import pallas as pl
from jax.experimental.pallas import tpu as pltpu

F32 = jnp.float32
BF16 = jnp.bfloat16
U32 = jnp.uint32

D_MODEL = 1024
D_CONV = D_MODEL
CONV_K = 31
D_ML = 2 * D_MODEL
ML_HEADS = 4
ML_HEAD_DIM = D_ML // ML_HEADS
QK_CONV_K = 4
N_META = 16
EPS = 1e-6

LANES = 128
SUBLANES = 8
GATE_W = LANES
N_IF = 2 * ML_HEADS
CONV_HIST = 32
CONV_BASE = CONV_HIST - (CONV_K - 1)
CONV_SPAN = (CONV_BASE + CONV_K - 2) // SUBLANES * SUBLANES
QK_HIST = SUBLANES
QK_BASE = QK_HIST - (QK_CONV_K - 1)
QK_CHUNK = ML_HEAD_DIM
CONV_ROWS = 128
GLU_CHUNK = 2 * LANES
TN = 1024
VMEM_LIMIT = 62 * 1024 * 1024

D_PA = 3 * D_CONV
D_PB = 5 * D_ML
LO_IF = D_PA + D_PB
LO_GC = LO_IF + N_IF
LO_GM = LO_GC + D_MODEL
CA_A, CA_GL, CA_ZC = 0, D_CONV, 2 * D_CONV
CB_Q, CB_K, CB_V, CB_O, CB_ZM = 0, D_ML, 2 * D_ML, 3 * D_ML, 4 * D_ML


def _sigmoid(x):
    return jax.nn.sigmoid(x)


def _silu(x):
    return x * jax.nn.sigmoid(x)


def _rms(x, g):
    return x * lax.rsqrt(jnp.mean(x * x, axis=-1, keepdims=True) + EPS) * g


def _words(x_bf16):
    return pltpu.bitcast(x_bf16, U32)


def _halves(x_u32):
    return pltpu.bitcast(x_u32, BF16)


def _wdot(x_bf16, w_words):
    return jnp.dot(x_bf16, _halves(w_words), preferred_element_type=F32)


def _wdot_t(x_bf16, wt_words):
    return lax.dot_general(x_bf16, _halves(wt_words), (((1,), (1,)), ((), ())),
                           preferred_element_type=F32)


def _params(n_grid):
    return pltpu.CompilerParams(dimension_semantics=("arbitrary",) * n_grid,
                                vmem_limit_bytes=VMEM_LIMIT)


def _pack_kernel(w_ref, o_ref):
    o_ref[...] = _words(w_ref[...].astype(BF16))


def _pack(w3d, layer, col_lo=0, n_col=None):
    depth, k, n = w3d.shape
    n_col = n - col_lo if n_col is None else n_col
    tn = min(n_col, TN)
    assert col_lo % tn == 0 and n_col % tn == 0
    j0 = col_lo // tn
    return pl.pallas_call(
        _pack_kernel,
        grid=(n_col // tn,),
        in_specs=[pl.BlockSpec((k, tn), lambda j: (layer, j + j0))],
        out_specs=pl.BlockSpec((k // 2, tn), lambda j: (0, j)),
        out_shape=jax.ShapeDtypeStruct((k // 2, n_col), U32),
        compiler_params=_params(1),
        name="pack_weight",
    )(w3d.reshape(depth * k, n))


def _pack_t(wt3d, layer, row_lo=0, n_row=None):
    _, n, k = wt3d.shape
    n_row = n - row_lo if n_row is None else n_row
    tn = min(n_row, TN)
    assert row_lo % tn == 0 and n_row % tn == 0
    j0 = row_lo // tn
    return pl.pallas_call(
        _pack_kernel,
        grid=(n_row // tn,),
        in_specs=[pl.BlockSpec((None, tn, k), lambda j: (layer, j + j0, 0))],
        out_specs=pl.BlockSpec((tn // 2, k), lambda j: (j, 0)),
        out_shape=jax.ShapeDtypeStruct((n_row // 2, k), U32),
        compiler_params=_params(1),
        name="pack_weight_t",
    )(wt3d)


def _inproj_kernel(*refs, gated):
    if gated:
        x_ref, g_ref, w_ref, b_ref, wg_ref, bg_ref, o_ref, og_ref, h_ref = refs
    else:
        x_ref, g_ref, w_ref, b_ref, o_ref, h_ref = refs

    @pl.when(pl.program_id(1) == 0)
    def _():
        hb = _rms(x_ref[...], g_ref[...]).astype(BF16)
        h_ref[...] = _words(hb)
        if gated:
            og_ref[...] = _wdot_t(hb, wg_ref[...]) + bg_ref[...]

    o_ref[...] = _wdot_t(_halves(h_ref[...]), w_ref[...]) + b_ref[...]


def _inproj(x2d, g, wt_words, b, tm, w_gate=None, b_gate=None):
    n_tok, n_col = x2d.shape[0], 2 * wt_words.shape[0]
    gated = w_gate is not None
    in_specs = [
        pl.BlockSpec((tm, D_MODEL), lambda i, j: (i, 0)),
        pl.BlockSpec((1, D_MODEL), lambda i, j: (0, 0)),
        pl.BlockSpec((TN // 2, D_MODEL), lambda i, j: (j, 0)),
        pl.BlockSpec((1, TN), lambda i, j: (0, j)),
    ]
    out_specs = [pl.BlockSpec((tm, TN), lambda i, j: (i, j))]
    out_shape = [jax.ShapeDtypeStruct((n_tok, n_col), F32)]
    args = [x2d, g, wt_words, b]
    if gated:
        in_specs += [pl.BlockSpec((GATE_W // 2, D_MODEL), lambda i, j: (0, 0)),
                     pl.BlockSpec((1, GATE_W), lambda i, j: (0, 0))]
        out_specs.append(pl.BlockSpec((tm, GATE_W), lambda i, j: (i, 0)))
        out_shape.append(jax.ShapeDtypeStruct((n_tok, GATE_W), F32))
        args += [w_gate, b_gate]
    return pl.pallas_call(
        functools.partial(_inproj_kernel, gated=gated),
        grid=(n_tok // tm, n_col // TN),
        in_specs=in_specs, out_specs=out_specs, out_shape=out_shape,
        scratch_shapes=[pltpu.VMEM((tm // 2, D_MODEL), U32)],
        compiler_params=_params(2),
        name="inproj",
    )(*args)


def _cumsum_rows(x, n_rows):
    row = lax.broadcasted_iota(jnp.int32, x.shape, 0)
    k = 1
    while k < n_rows:
        x = x + jnp.where(row >= k, pltpu.roll(x, k, axis=0), 0.0)
        k *= 2
    return x


def _conv_kernel(*refs, L, pad, fresh, first_block, chained):
    refs = list(refs)
    x_ref = refs.pop(0) if fresh else None
    meta_ref = refs.pop(0) if first_block else None
    if fresh:
        npre, wpa, bpa, wgc, bgc = (refs.pop(0) for _ in range(5))
    else:
        pa_ref, pgc_ref, cb_in = (refs.pop(0) for _ in range(3))
    wdw, bdw, lng, lnb, wco = (refs.pop(0) for _ in range(5))
    if chained:
        refs.pop(0)
    gcb_ref, cb_out = refs.pop(0), refs.pop(0)
    ubuf, ush, cvo = (refs.pop(0) for _ in range(3))

    c = pl.program_id(1)

    @pl.when(c == 0)
    def _init():
        ubuf[0:CONV_HIST, :] = jnp.zeros((CONV_HIST, D_CONV), F32)
        if not fresh:
            ubuf[CONV_BASE:CONV_HIST, :] = cb_in[0]

    if fresh:
        x = x_ref[0]
        if first_block:
            x0 = jnp.concatenate([jnp.zeros((pad, D_MODEL), F32), meta_ref[...]], axis=0)
            x = jnp.where(c == 0, x0, x)
        h = _rms(x, npre[...]).astype(BF16)

        def proj_a(off, width):
            return _wdot_t(h, wpa[off // 2:(off + width) // 2, :]) + bpa[:, off:off + width]

        def proj_gc(off, width):
            return _wdot_t(h, wgc[off // 2:(off + width) // 2, :]) + bgc[:, off:off + width]
    else:
        def proj_a(off, width):
            return pa_ref[0, :, off:off + width]

        def proj_gc(off, width):
            return pgc_ref[0, :, off:off + width]

    if pad:
        row_id = lax.broadcasted_iota(jnp.int32, (L, 1), 0) + c * L
        valid = row_id >= pad

    rc = min(L, CONV_ROWS)
    n_chunk = D_CONV // GLU_CHUNK
    for ck in range(n_chunk):
        co = ck * GLU_CHUNK
        u = proj_a(CA_A + co, GLU_CHUNK) * _sigmoid(proj_a(CA_GL + co, GLU_CHUNK))
        if pad:
            u = jnp.where(valid, u, 0.0)
        ubuf[CONV_HIST:CONV_HIST + L, co:co + GLU_CHUNK] = u
        if ck == n_chunk - 2:
            zc = proj_a(CA_ZC, D_CONV)
        if ck == n_chunk - 1:
            gc = proj_gc(0, D_MODEL)
        for it in range(co // LANES, (co + GLU_CHUNK) // LANES):
            ls = slice(it * LANES, (it + 1) * LANES)
            sb = (it % 2) * (SUBLANES - 1)
            for sh in range(1, SUBLANES):
                ush[sb + sh - 1, :, :] = ubuf[sh:sh + L + CONV_SPAN, ls]
            for r0 in range(0, L, rc):
                acc = jnp.zeros((rc, LANES), F32) + bdw[:, ls]
                for j in range(CONV_K):
                    a8, sh = divmod(CONV_BASE + j, SUBLANES)
                    lo = a8 * SUBLANES + r0
                    src = ubuf[lo:lo + rc, ls] if sh == 0 else ush[sb + sh - 1, lo:lo + rc, :]
                    acc = acc + src * wdw[j:j + 1, ls]
                cvo[r0:r0 + rc, ls] = acc
    cb_out[0] = ubuf[L + CONV_BASE:L + CONV_HIST, :]
    ubuf[0:CONV_HIST, :] = ubuf[L:L + CONV_HIST, :]

    cv = cvo[...]
    mu = jnp.mean(cv, axis=-1, keepdims=True)
    xc = cv - mu
    ln = xc * lax.rsqrt(jnp.mean(xc * xc, axis=-1, keepdims=True) + EPS) * lng[...] + lnb[...]
    ua = _silu(ln) * _silu(zc)
    br_c = _wdot(ua.astype(BF16), wco[...])
    gcb_ref[0] = _sigmoid(gc) * br_c


def _conv_branch(x, lw, layer, depth, L, *, meta=None, proj=None, state=None, prev_out=None, pad=0):
    fresh = proj is None
    first_block = meta is not None
    chained = prev_out is not None
    src = x if fresh else proj[0]
    S = src.shape[0]
    T = src.shape[1] + (L if first_block else 0)
    shift = lambda c: jnp.maximum(c - 1, 0)
    blk = lambda s, c: (s, c, 0)
    full2 = lambda s, c: (0, 0)
    const = dict(pipeline_mode=pl.Buffered(1))
    st_shape = (depth, S, CONV_K - 1, D_CONV)
    st_spec = pl.BlockSpec((None, 1) + st_shape[2:], lambda s, c: (layer, s, 0, 0))

    in_specs, args = [], []
    if fresh:
        in_specs.append(pl.BlockSpec((1, L, D_MODEL), (lambda s, c: (s, shift(c), 0)) if first_block
                                     else blk))
        args.append(x)
        if first_block:
            in_specs.append(pl.BlockSpec(meta.shape, full2, **const))
            args.append(meta)
        ws = [lw["norm_pre"], lw["w_pa"], lw["b_pa"], lw["w_gc"], lw["b_gc"]]
    else:
        in_specs += [pl.BlockSpec((1, L, D_PA), blk), pl.BlockSpec((1, L, D_MODEL), blk), st_spec]
        args += [proj[0], proj[1], state]
        ws = []
    ws += [lw["w_dw"], lw["b_dw"], lw["ln_g"], lw["ln_b"], lw["w_conv_out"]]
    in_specs += [pl.BlockSpec(w.shape, full2, **const) for w in ws]
    args += ws
    aliases = {}
    if chained:
        aliases[len(args)] = 1
        in_specs.append(pl.BlockSpec(memory_space=pl.ANY))
        args.append(prev_out)

    scratch = [
        pltpu.VMEM((L + CONV_HIST, D_CONV), F32),
        pltpu.VMEM((2 * (SUBLANES - 1), L + CONV_SPAN, LANES), F32),
        pltpu.VMEM((L, D_CONV), F32),
    ]
    return pl.pallas_call(
        functools.partial(_conv_kernel, L=L, pad=pad, fresh=fresh, first_block=first_block,
                          chained=chained),
        grid=(S, T // L),
        in_specs=in_specs,
        out_specs=[pl.BlockSpec((1, L, D_MODEL), blk), st_spec],
        out_shape=[jax.ShapeDtypeStruct((S, T, D_MODEL), F32), jax.ShapeDtypeStruct(st_shape, F32)],
        scratch_shapes=scratch,
        input_output_aliases=aliases,
        compiler_params=_params(2),
        name="conv_fresh" if fresh else "conv_carry",
    )(*args)


N_MSTATE = 4


def _mlstm_kernel(*refs, L, pad, fresh, first_block, chained):
    refs = list(refs)
    x_ref = refs.pop(0)
    meta_ref = refs.pop(0) if first_block else None
    gcb_ref = refs.pop(0)
    if fresh:
        npre, wpb, bpb, wgm, bgm, wgate, bgate = (refs.pop(0) for _ in range(7))
    else:
        pb_ref, pgm_ref, pg_ref = (refs.pop(0) for _ in range(3))
        qb_in, c_in, n_in, m_in = (refs.pop(0) for _ in range(N_MSTATE))
    wqk, bqk, fb, mln, wmo, wo, npost = (refs.pop(0) for _ in range(7))
    if chained:
        del refs[:N_MSTATE]
    y_ref, qb_out, c_out, n_out, m_out = (refs.pop(0) for _ in range(1 + N_MSTATE))
    qkc, qkh, qks = (refs.pop(0) for _ in range(3))
    cw = refs.pop(0) if fresh else None

    c = pl.program_id(1)

    @pl.when(c == 0)
    def _init():
        qkh[...] = jnp.zeros(qkh.shape, F32)
        if fresh:
            c_out[...] = jnp.zeros(c_out.shape, F32)
            n_out[...] = jnp.zeros(n_out.shape, F32)
            m_out[...] = jnp.zeros(m_out.shape, F32)
            cw[...] = jnp.zeros(cw.shape, U32)
        else:
            qkh[QK_BASE:QK_HIST, :] = qb_in[0]

    if fresh:
        m_prev = m_out[0]
        c_prev = lambda h_i: c_out[0, h_i]
        c_prev_bf16 = lambda h_i: _halves(cw[h_i])
        n_prev = lambda h_i: n_out[0, h_i:h_i + 1, :]
    else:
        m_prev = m_in[0]
        c_prev = lambda h_i: c_in[0, h_i]
        c_prev_bf16 = lambda h_i: c_in[0, h_i].astype(BF16)
        n_prev = lambda h_i: n_in[0, h_i:h_i + 1, :]

    def x_block():
        x = x_ref[0]
        if first_block:
            x0 = jnp.concatenate([jnp.zeros((pad, D_MODEL), F32), meta_ref[...]], axis=0)
            x = jnp.where(c == 0, x0, x)
        return x

    if fresh:
        h = _rms(x_block(), npre[...]).astype(BF16)

        def proj(off, width):
            return _wdot_t(h, wpb[off // 2:(off + width) // 2, :]) + bpb[:, off:off + width]

        g = _wdot_t(h, wgate[...]) + bgate[...]
    else:
        def proj(off, width):
            return pb_ref[0, :, off:off + width]

        g = pg_ref[0]

    row_id = lax.broadcasted_iota(jnp.int32, (L, 1), 0) + c * L
    valid = row_id >= pad

    def qk_chunk(it):
        cs = slice(it * QK_CHUNK, (it + 1) * QK_CHUNK)
        qk_pre = proj(CB_Q + it * QK_CHUNK, QK_CHUNK)
        if pad:
            qk_pre = jnp.where(valid, qk_pre, 0.0)
        qkc[0:QK_HIST, :] = qkh[:, cs]
        qkc[QK_HIST:QK_HIST + L, :] = qk_pre
        acc = qk_pre * wqk[QK_CONV_K - 1:QK_CONV_K, cs] + bqk[:, cs]
        for j in range(QK_CONV_K - 1):
            acc = acc + qkc[QK_BASE + j:QK_BASE + j + L, :] * wqk[j:j + 1, cs]
        qkh[:, cs] = qkc[L:L + QK_HIST, :]
        acc = _silu(acc)
        if it * QK_CHUNK >= D_ML:
            acc = acc * (ML_HEAD_DIM ** -0.5)
        qks[:, cs] = _words(acc.astype(BF16))

    for it in range(2 * D_ML // QK_CHUNK):
        qk_chunk(it)
    qb_out[0] = qkh[QK_BASE:QK_HIST, :]

    logf = jax.nn.log_sigmoid(g + fb[...])
    if pad:
        logf = jnp.where(valid, logf, 0.0)
    b_al = pltpu.roll(_cumsum_rows(logf, L), GATE_W - ML_HEADS, axis=1)
    inter = b_al + m_prev
    d_src = g - b_al
    d_src_t = d_src.T
    tt = lax.broadcasted_iota(jnp.int32, (L, L), 0)
    ss = lax.broadcasted_iota(jnp.int32, (L, L), 1)
    mask = ss <= tt
    if pad:
        mask = mask & ((ss + c * L) >= pad)

    hms, updates = [], []
    for h_i in range(ML_HEADS):
        hs = slice(h_i * ML_HEAD_DIM, (h_i + 1) * ML_HEAD_DIM)
        v = proj(CB_V + h_i * ML_HEAD_DIM, ML_HEAD_DIM)
        og = proj(CB_O + h_i * ML_HEAD_DIM, ML_HEAD_DIM)
        zm = proj(CB_ZM + h_i * ML_HEAD_DIM, ML_HEAD_DIM)
        q = _halves(qks[:, hs])
        k = _halves(qks[:, D_ML + h_i * ML_HEAD_DIM:D_ML + (h_i + 1) * ML_HEAD_DIM])
        dmat = jnp.where(mask, b_al[:, h_i:h_i + 1] + d_src_t[h_i:h_i + 1, :], -jnp.inf)
        inter_h = inter[:, h_i:h_i + 1]
        m_t = jnp.maximum(inter_h, jnp.max(dmat, axis=-1, keepdims=True))
        w_intra = jnp.exp(dmat - m_t)
        w_inter = jnp.exp(inter_h - m_t)
        s_mat = lax.dot_general(q, k, (((1,), (1,)), ((), ())), preferred_element_type=F32) * w_intra
        n_old = n_prev(h_i)
        qc = lax.dot_general(q, c_prev_bf16(h_i), (((1,), (1,)), ((), ())), preferred_element_type=F32)
        num = jnp.dot(s_mat.astype(BF16), v.astype(BF16), preferred_element_type=F32) + w_inter * qc
        qn = jnp.sum(q.astype(F32) * n_old, axis=-1, keepdims=True)
        den = jnp.sum(s_mat, axis=-1, keepdims=True) + w_inter * qn
        hh = num * (1.0 / jnp.maximum(jnp.abs(den), jnp.exp(-m_t)))
        hh = hh * lax.rsqrt(jnp.mean(hh * hh, axis=-1, keepdims=True) + EPS) * mln[:, hs]
        hm = hh * _sigmoid(og) * _silu(zm)
        hms.append(hm.astype(BF16))

        m_new = m_t[L - 1:L, :]
        decay = jnp.exp(b_al[L - 1:L, h_i:h_i + 1] + m_prev[:, h_i:h_i + 1] - m_new)
        w_s = jnp.exp(d_src[:, h_i:h_i + 1] + (b_al[L - 1:L, h_i:h_i + 1] - m_new))
        if pad:
            w_s = jnp.where(valid, w_s, 0.0)
        vw = (w_s * v).astype(BF16)
        n_new = decay * n_old + jnp.sum(w_s * k.astype(F32), axis=0, keepdims=True)
        updates.append((m_new, decay, vw, k, n_new))

    if fresh:
        gm = _wdot_t(h, wgm[...]) + bgm[...]
    else:
        gm = pgm_ref[0]
    br_m = _wdot(jnp.concatenate(hms, axis=1), wmo[...])
    ym = gcb_ref[0] + _sigmoid(gm) * br_m
    out = _wdot(ym.astype(BF16), wo[...])
    y_ref[0] = x_block() + _rms(out, npost[...])

    lane = lax.broadcasted_iota(jnp.int32, (1, GATE_W), 1)
    m_row = jnp.zeros((1, GATE_W), F32)
    for h_i, (m_new, decay, vw, k, n_new) in enumerate(updates):
        c_new = decay * c_prev(h_i) + lax.dot_general(
            vw, k, (((0,), (0,)), ((), ())), preferred_element_type=F32)
        c_out[0, h_i] = c_new
        if fresh:
            cw[h_i] = _words(c_new.astype(BF16))
        n_out[0, h_i:h_i + 1, :] = n_new
        m_row = jnp.where(lane == h_i, m_new, m_row)
    m_out[0] = m_row


def _mstate_shapes(depth, S):
    return [
        (depth, S, QK_CONV_K - 1, 2 * D_ML),
        (depth, S, ML_HEADS, ML_HEAD_DIM, ML_HEAD_DIM),
        (depth, S, ML_HEADS, ML_HEAD_DIM),
        (depth, S, 1, GATE_W),
    ]


def _mlstm_branch(x, gcb, lw, layer, depth, L, *, meta=None, proj=None, state=None, prev_out=None,
                  pad=0, drop_first=False):
    fresh = proj is None
    first_block = meta is not None
    chained = prev_out is not None
    S, T = gcb.shape[0], gcb.shape[1]
    assert fresh or T == L
    shapes = _mstate_shapes(depth, S)
    shift = lambda c: jnp.maximum(c - 1, 0)
    blk = lambda s, c: (s, c, 0)
    full2 = lambda s, c: (0, 0)
    const = dict(pipeline_mode=pl.Buffered(1))

    def state_spec(shape, **kw):
        nd = len(shape) - 2
        return pl.BlockSpec((None, 1) + shape[2:], lambda s, c: (layer, s) + (0,) * nd, **kw)

    in_specs = [pl.BlockSpec((1, L, D_MODEL), (lambda s, c: (s, shift(c), 0)) if first_block else blk)]
    args = [x]
    if first_block:
        in_specs.append(pl.BlockSpec(meta.shape, full2, **const))
        args.append(meta)
    in_specs.append(pl.BlockSpec((1, L, D_MODEL), blk))
    args.append(gcb)
    if fresh:
        ws = [lw["norm_pre"], lw["w_pb"], lw["b_pb"], lw["w_gm"], lw["b_gm"], lw["w_gate"], lw["b_gate"]]
    else:
        in_specs += [pl.BlockSpec((1, L, D_PB), blk), pl.BlockSpec((1, L, D_MODEL), blk),
                     pl.BlockSpec((1, L, GATE_W), blk)]
        args += list(proj)
        in_specs += [state_spec(sh) for sh in shapes]
        args += list(state)
        ws = []
    ws += [lw["w_qkc"], lw["b_qkc"], lw["f_bias"], lw["ml_norm"], lw["w_ml_out"], lw["w_out"],
           lw["norm_post"]]
    in_specs += [pl.BlockSpec(w.shape, full2, **const) for w in ws]
    args += ws
    aliases = {}
    if chained:
        for k, a in enumerate(prev_out):
            aliases[len(args)] = 1 + k
            in_specs.append(pl.BlockSpec(memory_space=pl.ANY))
            args.append(a)

    t_out = T - L if drop_first else T
    out_shape = [jax.ShapeDtypeStruct((S, t_out, D_MODEL), F32)] + \
        [jax.ShapeDtypeStruct(sh, F32) for sh in shapes]
    out_specs = [pl.BlockSpec((1, L, D_MODEL), (lambda s, c: (s, shift(c), 0)) if drop_first else blk)] + \
        [state_spec(sh) for sh in shapes]
    scratch = [
        pltpu.VMEM((L + QK_HIST, QK_CHUNK), F32),
        pltpu.VMEM((QK_HIST, 2 * D_ML), F32),
        pltpu.VMEM((L // 2, 2 * D_ML), U32),
    ]
    if fresh:
        scratch.append(pltpu.VMEM((ML_HEADS, ML_HEAD_DIM // 2, ML_HEAD_DIM), U32))
    outs = pl.pallas_call(
        functools.partial(_mlstm_kernel, L=L, pad=pad, fresh=fresh, first_block=first_block,
                          chained=chained),
        grid=(S, T // L),
        in_specs=in_specs,
        out_specs=out_specs,
        out_shape=out_shape,
        scratch_shapes=scratch,
        input_output_aliases=aliases,
        compiler_params=_params(2),
        name="mlstm_fresh" if fresh else "mlstm_carry",
    )(*args)
    return outs[0], outs[1:]


L_PROMPT = 256
TM_SAMPLE = 1024


def _layer_weights(l, norm_pre, norm_post, w_in_t, b_in, w_dw, b_dw, ln_g, ln_b, w_qk_conv, b_qk_conv,
                   f_bias, ml_norm, w_conv_out, w_ml_out, w_out):
    b = b_in[l]
    tail = w_in_t[l, LO_IF:, :]
    w_gate = jnp.pad(tail[:N_IF], ((0, GATE_W - N_IF), (0, 0)))[None]
    w_gc = tail[N_IF:N_IF + D_MODEL][None]
    w_gm = tail[N_IF + D_MODEL:][None]
    row = lambda a: a[None, :]
    return dict(
        norm_pre=row(norm_pre[l]), norm_post=row(norm_post[l]),
        w_pa=_pack_t(w_in_t, l, 0, D_PA), b_pa=row(b[:D_PA]),
        w_pb=_pack_t(w_in_t, l, D_PA, D_PB), b_pb=row(b[D_PA:LO_IF]),
        w_gate=_pack_t(w_gate, 0), b_gate=row(jnp.pad(b[LO_IF:LO_GC], (0, GATE_W - N_IF))),
        w_gc=_pack_t(w_gc, 0), b_gc=row(b[LO_GC:LO_GM]),
        w_gm=_pack_t(w_gm, 0), b_gm=row(b[LO_GM:]),
        w_dw=w_dw[l], b_dw=row(b_dw[l]), ln_g=row(ln_g[l]), ln_b=row(ln_b[l]),
        w_qkc=w_qk_conv[l], b_qkc=row(b_qk_conv[l]),
        f_bias=row(jnp.pad(f_bias[l], (ML_HEADS, GATE_W - N_IF))), ml_norm=row(ml_norm[l]),
        w_conv_out=_pack(w_conv_out, l), w_ml_out=_pack(w_ml_out, l), w_out=_pack(w_out, l))


def kernel(x_prompt, x_sample, state_conv, state_qk_conv, state_C, state_n, state_m, meta_tokens,
           norm_pre, norm_post, w_in, b_in, w_dw, b_dw, ln_g, ln_b, w_qk_conv, b_qk_conv, f_bias,
           ml_norm, w_conv_out, w_ml_out, w_out):
    bp, seq, _ = x_prompt.shape
    bs, dec_seq, _ = x_sample.shape
    depth = w_in.shape[0]
    L = L_PROMPT
    pad = L - N_META
    assert seq % L == 0 and pad % SUBLANES == 0 and dec_seq % (2 * SUBLANES) == 0

    meta = meta_tokens.astype(x_prompt.dtype)
    m_pad = jnp.pad(state_m, ((0, 0), (0, 0), (0, GATE_W - ML_HEADS)))[:, :, None, :]
    w_in_t = jnp.swapaxes(w_in, 1, 2)
    mstate_in = (state_qk_conv, state_C, state_n, m_pad)

    xp, xs = x_prompt, x_sample
    cb_p = cb_s = st_p = st_s = None
    for l in range(depth):
        lw = _layer_weights(l, norm_pre, norm_post, w_in_t, b_in, w_dw, b_dw, ln_g, ln_b, w_qk_conv,
                            b_qk_conv, f_bias, ml_norm, w_conv_out, w_ml_out, w_out)
        first = meta if l == 0 else None
        gcb, cb_p = _conv_branch(xp, lw, l, depth, L, meta=first, prev_out=cb_p, pad=pad)
        xp, st_p = _mlstm_branch(xp, gcb, lw, l, depth, L, meta=first, prev_out=st_p, pad=pad,
                                 drop_first=(l == depth - 1))

        xs2 = xs.reshape(bs * dec_seq, D_MODEL)
        tok = lambda a: a.reshape(bs, dec_seq, a.shape[-1])
        (pa,) = _inproj(xs2, lw["norm_pre"], lw["w_pa"], lw["b_pa"], TM_SAMPLE)
        (pgc,) = _inproj(xs2, lw["norm_pre"], lw["w_gc"], lw["b_gc"], TM_SAMPLE)
        pb, pg = _inproj(xs2, lw["norm_pre"], lw["w_pb"], lw["b_pb"], TM_SAMPLE, lw["w_gate"], lw["b_gate"])
        (pgm,) = _inproj(xs2, lw["norm_pre"], lw["w_gm"], lw["b_gm"], TM_SAMPLE)
        gcb, cb_s = _conv_branch(None, lw, l, depth, dec_seq, proj=(tok(pa), tok(pgc)),
                                 state=state_conv, prev_out=cb_s)
        xs, st_s = _mlstm_branch(xs, gcb, lw, l, depth, dec_seq, proj=(tok(pb), tok(pgm), tok(pg)),
                                 state=mstate_in, prev_out=st_s)

    fin = lambda cb, st: (cb,) + tuple(st[:3]) + (st[3][:, :, 0, :ML_HEADS],)
    return (xp, xs) + fin(cb_p, st_p) + fin(cb_s, st_s)
```

```python
import functools

import jax
import jax.numpy as jnp
from jax import lax
from jax.experimental import pallas as pl
from jax.experimental.pallas import tpu as pltpu

F32 = jnp.float32
BF16 = jnp.bfloat16
U32 = jnp.uint32

D_MODEL = 1024
D_CONV = D_MODEL
CONV_K = 31
D_ML = 2 * D_MODEL
ML_HEADS = 4
ML_HEAD_DIM = D_ML // ML_HEADS
QK_CONV_K = 4
N_META = 16
EPS = 1e-6

LANES = 128
SUBLANES = 8
GATE_W = LANES
N_IF = 2 * ML_HEADS
CONV_HIST = 32
CONV_BASE = CONV_HIST - (CONV_K - 1)
CONV_SPAN = (CONV_BASE + CONV_K - 2) // SUBLANES * SUBLANES
QK_HIST = SUBLANES
QK_BASE = QK_HIST - (QK_CONV_K - 1)
QK_CHUNK = D_ML
CONV_ROWS = 128
GLU_CHUNK = 2 * LANES
TN = 1024
VMEM_LIMIT = 62 * 1024 * 1024

D_PA = 3 * D_CONV
D_PB = 5 * D_ML
LO_IF = D_PA + D_PB
LO_GC = LO_IF + N_IF
LO_GM = LO_GC + D_MODEL
CA_A, CA_GL, CA_ZC = 0, D_CONV, 2 * D_CONV
CB_Q, CB_K, CB_V, CB_O, CB_ZM = 0, D_ML, 2 * D_ML, 3 * D_ML, 4 * D_ML


def _sigmoid(x):
    return jax.nn.sigmoid(x)


def _silu(x):
    return x * jax.nn.sigmoid(x)


def _rms(x, g):
    return x * lax.rsqrt(jnp.mean(x * x, axis=-1, keepdims=True) + EPS) * g


def _words(x_bf16):
    return pltpu.bitcast(x_bf16, U32)


def _halves(x_u32):
    return pltpu.bitcast(x_u32, BF16)


def _wdot(x_bf16, w_words):
    return jnp.dot(x_bf16, _halves(w_words), preferred_element_type=F32)


def _wdot_t(x_bf16, wt_words):
    return lax.dot_general(x_bf16, _halves(wt_words), (((1,), (1,)), ((), ())),
                           preferred_element_type=F32)


def _params(n_grid):
    return pltpu.CompilerParams(dimension_semantics=("arbitrary",) * n_grid,
                                vmem_limit_bytes=VMEM_LIMIT)


def _pack_kernel(w_ref, o_ref):
    o_ref[...] = _words(w_ref[...].astype(BF16))


def _pack(w3d, layer, col_lo=0, n_col=None):
    depth, k, n = w3d.shape
    n_col = n - col_lo if n_col is None else n_col
    tn = min(n_col, TN)
    assert col_lo % tn == 0 and n_col % tn == 0
    j0 = col_lo // tn
    return pl.pallas_call(
        _pack_kernel,
        grid=(n_col // tn,),
        in_specs=[pl.BlockSpec((k, tn), lambda j: (layer, j + j0))],
        out_specs=pl.BlockSpec((k // 2, tn), lambda j: (0, j)),
        out_shape=jax.ShapeDtypeStruct((k // 2, n_col), U32),
        compiler_params=_params(1),
        name="pack_weight",
    )(w3d.reshape(depth * k, n))


def _pack_t(wt3d, layer, row_lo=0, n_row=None):
    _, n, k = wt3d.shape
    n_row = n - row_lo if n_row is None else n_row
    tn = min(n_row, TN)
    assert row_lo % tn == 0 and n_row % tn == 0
    j0 = row_lo // tn
    return pl.pallas_call(
        _pack_kernel,
        grid=(n_row // tn,),
        in_specs=[pl.BlockSpec((None, tn, k), lambda j: (layer, j + j0, 0))],
        out_specs=pl.BlockSpec((tn // 2, k), lambda j: (j, 0)),
        out_shape=jax.ShapeDtypeStruct((n_row // 2, k), U32),
        compiler_params=_params(1),
        name="pack_weight_t",
    )(wt3d)


def _inproj_kernel(*refs, gated):
    if gated:
        x_ref, g_ref, w_ref, b_ref, wg_ref, bg_ref, o_ref, og_ref, h_ref = refs
    else:
        x_ref, g_ref, w_ref, b_ref, o_ref, h_ref = refs

    @pl.when(pl.program_id(1) == 0)
    def _():
        hb = _rms(x_ref[...], g_ref[...]).astype(BF16)
        h_ref[...] = _words(hb)
        if gated:
            og_ref[...] = _wdot_t(hb, wg_ref[...]) + bg_ref[...]

    o_ref[...] = _wdot_t(_halves(h_ref[...]), w_ref[...]) + b_ref[...]


def _inproj(x2d, g, wt_words, b, tm, w_gate=None, b_gate=None):
    n_tok, n_col = x2d.shape[0], 2 * wt_words.shape[0]
    gated = w_gate is not None
    in_specs = [
        pl.BlockSpec((tm, D_MODEL), lambda i, j: (i, 0)),
        pl.BlockSpec((1, D_MODEL), lambda i, j: (0, 0)),
        pl.BlockSpec((TN // 2, D_MODEL), lambda i, j: (j, 0)),
        pl.BlockSpec((1, TN), lambda i, j: (0, j)),
    ]
    out_specs = [pl.BlockSpec((tm, TN), lambda i, j: (i, j))]
    out_shape = [jax.ShapeDtypeStruct((n_tok, n_col), F32)]
    args = [x2d, g, wt_words, b]
    if gated:
        in_specs += [pl.BlockSpec((GATE_W // 2, D_MODEL), lambda i, j: (0, 0)),
                     pl.BlockSpec((1, GATE_W), lambda i, j: (0, 0))]
        out_specs.append(pl.BlockSpec((tm, GATE_W), lambda i, j: (i, 0)))
        out_shape.append(jax.ShapeDtypeStruct((n_tok, GATE_W), F32))
        args += [w_gate, b_gate]
    return pl.pallas_call(
        functools.partial(_inproj_kernel, gated=gated),
        grid=(n_tok // tm, n_col // TN),
        in_specs=in_specs, out_specs=out_specs, out_shape=out_shape,
        scratch_shapes=[pltpu.VMEM((tm // 2, D_MODEL), U32)],
        compiler_params=_params(2),
        name="inproj",
    )(*args)


def _cumsum_rows(x, n_rows):
    row = lax.broadcasted_iota(jnp.int32, x.shape, 0)
    k = 1
    while k < n_rows:
        x = x + jnp.where(row >= k, pltpu.roll(x, k, axis=0), 0.0)
        k *= 2
    return x


def _conv_kernel(*refs, L, pad, fresh, first_block, chained):
    refs = list(refs)
    x_ref = refs.pop(0) if fresh else None
    meta_ref = refs.pop(0) if first_block else None
    if fresh:
        npre, wpa, bpa, wgc, bgc = (refs.pop(0) for _ in range(5))
    else:
        pa_ref, pgc_ref, cb_in = (refs.pop(0) for _ in range(3))
    wdw, bdw, lng, lnb, wco = (refs.pop(0) for _ in range(5))
    if chained:
        refs.pop(0)
    gcb_ref, cb_out = refs.pop(0), refs.pop(0)
    ubuf, ush, cvo = (refs.pop(0) for _ in range(3))

    c = pl.program_id(1)

    @pl.when(c == 0)
    def _init():
        ubuf[0:CONV_HIST, :] = jnp.zeros((CONV_HIST, D_CONV), F32)
        if not fresh:
            ubuf[CONV_BASE:CONV_HIST, :] = cb_in[0]

    if fresh:
        x = x_ref[0]
        if first_block:
            x0 = jnp.concatenate([jnp.zeros((pad, D_MODEL), F32), meta_ref[...]], axis=0)
            x = jnp.where(c == 0, x0, x)
        h = _rms(x, npre[...]).astype(BF16)

        def proj_a(off, width):
            return _wdot_t(h, wpa[off // 2:(off + width) // 2, :]) + bpa[:, off:off + width]

        def proj_gc(off, width):
            return _wdot_t(h, wgc[off // 2:(off + width) // 2, :]) + bgc[:, off:off + width]
    else:
        def proj_a(off, width):
            return pa_ref[0, :, off:off + width]

        def proj_gc(off, width):
            return pgc_ref[0, :, off:off + width]

    if pad:
        row_id = lax.broadcasted_iota(jnp.int32, (L, 1), 0) + c * L
        valid = row_id >= pad

    rc = min(L, CONV_ROWS)
    n_chunk = D_CONV // GLU_CHUNK
    for ck in range(n_chunk):
        co = ck * GLU_CHUNK
        u = proj_a(CA_A + co, GLU_CHUNK) * _sigmoid(proj_a(CA_GL + co, GLU_CHUNK))
        if pad:
            u = jnp.where(valid, u, 0.0)
        ubuf[CONV_HIST:CONV_HIST + L, co:co + GLU_CHUNK] = u
        if ck == n_chunk - 2:
            zc = proj_a(CA_ZC, D_CONV)
        if ck == n_chunk - 1:
            gc = proj_gc(0, D_MODEL)
        for it in range(co // LANES, (co + GLU_CHUNK) // LANES):
            ls = slice(it * LANES, (it + 1) * LANES)
            sb = (it % 2) * (SUBLANES - 1)
            for sh in range(1, SUBLANES):
                ush[sb + sh - 1, :, :] = ubuf[sh:sh + L + CONV_SPAN, ls]
            for r0 in range(0, L, rc):
                acc = jnp.zeros((rc, LANES), F32) + bdw[:, ls]
                for j in range(CONV_K):
                    a8, sh = divmod(CONV_BASE + j, SUBLANES)
                    lo = a8 * SUBLANES + r0
                    src = ubuf[lo:lo + rc, ls] if sh == 0 else ush[sb + sh - 1, lo:lo + rc, :]
                    acc = acc + src * wdw[j:j + 1, ls]
                cvo[r0:r0 + rc, ls] = acc
    cb_out[0] = ubuf[L + CONV_BASE:L + CONV_HIST, :]
    ubuf[0:CONV_HIST, :] = ubuf[L:L + CONV_HIST, :]

    cv = cvo[...]
    mu = jnp.mean(cv, axis=-1, keepdims=True)
    xc = cv - mu
    ln = xc * lax.rsqrt(jnp.mean(xc * xc, axis=-1, keepdims=True) + EPS) * lng[...] + lnb[...]
    ua = _silu(ln) * _silu(zc)
    br_c = _wdot(ua.astype(BF16), wco[...])
    gcb_ref[0] = _sigmoid(gc) * br_c


def _conv_branch(x, lw, layer, depth, L, *, meta=None, proj=None, state=None, prev_out=None, pad=0):
    fresh = proj is None
    first_block = meta is not None
    chained = prev_out is not None
    src = x if fresh else proj[0]
    S = src.shape[0]
    T = src.shape[1] + (L if first_block else 0)
    shift = lambda c: jnp.maximum(c - 1, 0)
    blk = lambda s, c: (s, c, 0)
    full2 = lambda s, c: (0, 0)
    const = dict(pipeline_mode=pl.Buffered(1))
    st_shape = (depth, S, CONV_K - 1, D_CONV)
    st_spec = pl.BlockSpec((None, 1) + st_shape[2:], lambda s, c: (layer, s, 0, 0))

    in_specs, args = [], []
    if fresh:
        in_specs.append(pl.BlockSpec((1, L, D_MODEL), (lambda s, c: (s, shift(c), 0)) if first_block
                                     else blk))
        args.append(x)
        if first_block:
            in_specs.append(pl.BlockSpec(meta.shape, full2, **const))
            args.append(meta)
        ws = [lw["norm_pre"], lw["w_pa"], lw["b_pa"], lw["w_gc"], lw["b_gc"]]
    else:
        in_specs += [pl.BlockSpec((1, L, D_PA), blk), pl.BlockSpec((1, L, D_MODEL), blk), st_spec]
        args += [proj[0], proj[1], state]
        ws = []
    ws += [lw["w_dw"], lw["b_dw"], lw["ln_g"], lw["ln_b"], lw["w_conv_out"]]
    in_specs += [pl.BlockSpec(w.shape, full2, **const) for w in ws]
    args += ws
    aliases = {}
    if chained:
        aliases[len(args)] = 1
        in_specs.append(pl.BlockSpec(memory_space=pl.ANY))
        args.append(prev_out)

    scratch = [
        pltpu.VMEM((L + CONV_HIST, D_CONV), F32),
        pltpu.VMEM((2 * (SUBLANES - 1), L + CONV_SPAN, LANES), F32),
        pltpu.VMEM((L, D_CONV), F32),
    ]
    return pl.pallas_call(
        functools.partial(_conv_kernel, L=L, pad=pad, fresh=fresh, first_block=first_block,
                          chained=chained),
        grid=(S, T // L),
        in_specs=in_specs,
        out_specs=[pl.BlockSpec((1, L, D_MODEL), blk), st_spec],
        out_shape=[jax.ShapeDtypeStruct((S, T, D_MODEL), F32), jax.ShapeDtypeStruct(st_shape, F32)],
        scratch_shapes=scratch,
        input_output_aliases=aliases,
        compiler_params=_params(2),
        name="conv_fresh" if fresh else "conv_carry",
    )(*args)


N_MSTATE = 4


def _mlstm_kernel(*refs, L, pad, fresh, first_block, chained):
    refs = list(refs)
    x_ref = refs.pop(0)
    meta_ref = refs.pop(0) if first_block else None
    gcb_ref = refs.pop(0)
    if fresh:
        npre, wpb, bpb, wgm, bgm, wgate, bgate = (refs.pop(0) for _ in range(7))
    else:
        pb_ref, pgm_ref, pg_ref = (refs.pop(0) for _ in range(3))
        qb_in, c_in, n_in, m_in = (refs.pop(0) for _ in range(N_MSTATE))
    wqk, bqk, fb, mln, wmo, wo, npost = (refs.pop(0) for _ in range(7))
    if chained:
        del refs[:N_MSTATE]
    y_ref, qb_out, c_out, n_out, m_out = (refs.pop(0) for _ in range(1 + N_MSTATE))
    qkc, qkh, qks = (refs.pop(0) for _ in range(3))
    cw = refs.pop(0) if fresh else None

    c = pl.program_id(1)

    @pl.when(c == 0)
    def _init():
        qkh[...] = jnp.zeros(qkh.shape, F32)
        if fresh:
            c_out[...] = jnp.zeros(c_out.shape, F32)
            n_out[...] = jnp.zeros(n_out.shape, F32)
            m_out[...] = jnp.zeros(m_out.shape, F32)
            cw[...] = jnp.zeros(cw.shape, U32)
        else:
            qkh[QK_BASE:QK_HIST, :] = qb_in[0]

    if fresh:
        m_prev = m_out[0]
        c_prev = lambda h_i: c_out[0, h_i]
        c_prev_bf16 = lambda h_i: _halves(cw[h_i])
        n_prev = lambda h_i: n_out[0, h_i:h_i + 1, :]
    else:
        m_prev = m_in[0]
        c_prev = lambda h_i: c_in[0, h_i]
        c_prev_bf16 = lambda h_i: c_in[0, h_i].astype(BF16)
        n_prev = lambda h_i: n_in[0, h_i:h_i + 1, :]

    def x_block():
        x = x_ref[0]
        if first_block:
            x0 = jnp.concatenate([jnp.zeros((pad, D_MODEL), F32), meta_ref[...]], axis=0)
            x = jnp.where(c == 0, x0, x)
        return x

    if fresh:
        h = _rms(x_block(), npre[...]).astype(BF16)

        def proj(off, width):
            return _wdot_t(h, wpb[off // 2:(off + width) // 2, :]) + bpb[:, off:off + width]

        g = _wdot_t(h, wgate[...]) + bgate[...]
    else:
        def proj(off, width):
            return pb_ref[0, :, off:off + width]

        g = pg_ref[0]

    row_id = lax.broadcasted_iota(jnp.int32, (L, 1), 0) + c * L
    valid = row_id >= pad

    def qk_chunk(it):
        cs = slice(it * QK_CHUNK, (it + 1) * QK_CHUNK)
        qk_pre = proj(CB_Q + it * QK_CHUNK, QK_CHUNK)
        if pad:
            qk_pre = jnp.where(valid, qk_pre, 0.0)
        qkc[0:QK_HIST, :] = qkh[:, cs]
        qkc[QK_HIST:QK_HIST + L, :] = qk_pre
        acc = qk_pre * wqk[QK_CONV_K - 1:QK_CONV_K, cs] + bqk[:, cs]
        for j in range(QK_CONV_K - 1):
            acc = acc + qkc[QK_BASE + j:QK_BASE + j + L, :] * wqk[j:j + 1, cs]
        qkh[:, cs] = qkc[L:L + QK_HIST, :]
        acc = _silu(acc)
        if it * QK_CHUNK >= D_ML:
            acc = acc * (ML_HEAD_DIM ** -0.5)
        qks[:, cs] = _words(acc.astype(BF16))

    for it in range(2 * D_ML // QK_CHUNK):
        qk_chunk(it)
    qb_out[0] = qkh[QK_BASE:QK_HIST, :]

    logf = jax.nn.log_sigmoid(g + fb[...])
    if pad:
        logf = jnp.where(valid, logf, 0.0)
    b_al = pltpu.roll(_cumsum_rows(logf, L), GATE_W - ML_HEADS, axis=1)
    inter = b_al + m_prev
    d_src = g - b_al
    d_src_t = d_src.T
    tt = lax.broadcasted_iota(jnp.int32, (L, L), 0)
    ss = lax.broadcasted_iota(jnp.int32, (L, L), 1)
    mask = ss <= tt
    if pad:
        mask = mask & ((ss + c * L) >= pad)

    hms, updates = [], []
    for h_i in range(ML_HEADS):
        hs = slice(h_i * ML_HEAD_DIM, (h_i + 1) * ML_HEAD_DIM)
        v = proj(CB_V + h_i * ML_HEAD_DIM, ML_HEAD_DIM)
        og = proj(CB_O + h_i * ML_HEAD_DIM, ML_HEAD_DIM)
        zm = proj(CB_ZM + h_i * ML_HEAD_DIM, ML_HEAD_DIM)
        q = _halves(qks[:, hs])
        k = _halves(qks[:, D_ML + h_i * ML_HEAD_DIM:D_ML + (h_i + 1) * ML_HEAD_DIM])
        dmat = jnp.where(mask, b_al[:, h_i:h_i + 1] + d_src_t[h_i:h_i + 1, :], -jnp.inf)
        inter_h = inter[:, h_i:h_i + 1]
        m_t = jnp.maximum(inter_h, jnp.max(dmat, axis=-1, keepdims=True))
        w_intra = jnp.exp(dmat - m_t)
        w_inter = jnp.exp(inter_h - m_t)
        s_mat = lax.dot_general(q, k, (((1,), (1,)), ((), ())), preferred_element_type=F32) * w_intra
        n_old = n_prev(h_i)
        qc = lax.dot_general(q, c_prev_bf16(h_i), (((1,), (1,)), ((), ())), preferred_element_type=F32)
        num = jnp.dot(s_mat.astype(BF16), v.astype(BF16), preferred_element_type=F32) + w_inter * qc
        qn = jnp.sum(q.astype(F32) * n_old, axis=-1, keepdims=True)
        den = jnp.sum(s_mat, axis=-1, keepdims=True) + w_inter * qn
        hh = num * (1.0 / jnp.maximum(jnp.abs(den), jnp.exp(-m_t)))
        hh = hh * lax.rsqrt(jnp.mean(hh * hh, axis=-1, keepdims=True) + EPS) * mln[:, hs]
        hm = hh * _sigmoid(og) * _silu(zm)
        hms.append(hm.astype(BF16))

        m_new = m_t[L - 1:L, :]
        decay = jnp.exp(b_al[L - 1:L, h_i:h_i + 1] + m_prev[:, h_i:h_i + 1] - m_new)
        w_s = jnp.exp(d_src[:, h_i:h_i + 1] + (b_al[L - 1:L, h_i:h_i + 1] - m_new))
        if pad:
            w_s = jnp.where(valid, w_s, 0.0)
        vw = (w_s * v).astype(BF16)
        n_new = decay * n_old + jnp.sum(w_s * k.astype(F32), axis=0, keepdims=True)
        updates.append((m_new, decay, vw, k, n_new))

    if fresh:
        gm = _wdot_t(h, wgm[...]) + bgm[...]
    else:
        gm = pgm_ref[0]
    br_m = _wdot(jnp.concatenate(hms, axis=1), wmo[...])
    ym = gcb_ref[0] + _sigmoid(gm) * br_m
    out = _wdot(ym.astype(BF16), wo[...])
    y_ref[0] = x_block() + _rms(out, npost[...])

    lane = lax.broadcasted_iota(jnp.int32, (1, GATE_W), 1)
    m_row = jnp.zeros((1, GATE_W), F32)
    for h_i, (m_new, decay, vw, k, n_new) in enumerate(updates):
        c_new = decay * c_prev(h_i) + lax.dot_general(
            vw, k, (((0,), (0,)), ((), ())), preferred_element_type=F32)
        c_out[0, h_i] = c_new
        if fresh:
            cw[h_i] = _words(c_new.astype(BF16))
        n_out[0, h_i:h_i + 1, :] = n_new
        m_row = jnp.where(lane == h_i, m_new, m_row)
    m_out[0] = m_row


def _mstate_shapes(depth, S):
    return [
        (depth, S, QK_CONV_K - 1, 2 * D_ML),
        (depth, S, ML_HEADS, ML_HEAD_DIM, ML_HEAD_DIM),
        (depth, S, ML_HEADS, ML_HEAD_DIM),
        (depth, S, 1, GATE_W),
    ]


def _mlstm_branch(x, gcb, lw, layer, depth, L, *, meta=None, proj=None, state=None, prev_out=None,
                  pad=0, drop_first=False):
    fresh = proj is None
    first_block = meta is not None
    chained = prev_out is not None
    S, T = gcb.shape[0], gcb.shape[1]
    assert fresh or T == L
    shapes = _mstate_shapes(depth, S)
    shift = lambda c: jnp.maximum(c - 1, 0)
    blk = lambda s, c: (s, c, 0)
    full2 = lambda s, c: (0, 0)
    const = dict(pipeline_mode=pl.Buffered(1))

    def state_spec(shape, **kw):
        nd = len(shape) - 2
        return pl.BlockSpec((None, 1) + shape[2:], lambda s, c: (layer, s) + (0,) * nd, **kw)

    in_specs = [pl.BlockSpec((1, L, D_MODEL), (lambda s, c: (s, shift(c), 0)) if first_block else blk)]
    args = [x]
    if first_block:
        in_specs.append(pl.BlockSpec(meta.shape, full2, **const))
        args.append(meta)
    in_specs.append(pl.BlockSpec((1, L, D_MODEL), blk))
    args.append(gcb)
    if fresh:
        ws = [lw["norm_pre"], lw["w_pb"], lw["b_pb"], lw["w_gm"], lw["b_gm"], lw["w_gate"], lw["b_gate"]]
    else:
        in_specs += [pl.BlockSpec((1, L, D_PB), blk), pl.BlockSpec((1, L, D_MODEL), blk),
                     pl.BlockSpec((1, L, GATE_W), blk)]
        args += list(proj)
        in_specs += [state_spec(sh) for sh in shapes]
        args += list(state)
        ws = []
    ws += [lw["w_qkc"], lw["b_qkc"], lw["f_bias"], lw["ml_norm"], lw["w_ml_out"], lw["w_out"],
           lw["norm_post"]]
    in_specs += [pl.BlockSpec(w.shape, full2, **const) for w in ws]
    args += ws
    aliases = {}
    if chained:
        for k, a in enumerate(prev_out):
            aliases[len(args)] = 1 + k
            in_specs.append(pl.BlockSpec(memory_space=pl.ANY))
            args.append(a)

    t_out = T - L if drop_first else T
    out_shape = [jax.ShapeDtypeStruct((S, t_out, D_MODEL), F32)] + \
        [jax.ShapeDtypeStruct(sh, F32) for sh in shapes]
    out_specs = [pl.BlockSpec((1, L, D_MODEL), (lambda s, c: (s, shift(c), 0)) if drop_first else blk)] + \
        [state_spec(sh) for sh in shapes]
    scratch = [
        pltpu.VMEM((L + QK_HIST, QK_CHUNK), F32),
        pltpu.VMEM((QK_HIST, 2 * D_ML), F32),
        pltpu.VMEM((L // 2, 2 * D_ML), U32),
    ]
    if fresh:
        scratch.append(pltpu.VMEM((ML_HEADS, ML_HEAD_DIM // 2, ML_HEAD_DIM), U32))
    outs = pl.pallas_call(
        functools.partial(_mlstm_kernel, L=L, pad=pad, fresh=fresh, first_block=first_block,
                          chained=chained),
        grid=(S, T // L),
        in_specs=in_specs,
        out_specs=out_specs,
        out_shape=out_shape,
        scratch_shapes=scratch,
        input_output_aliases=aliases,
        compiler_params=_params(2),
        name="mlstm_fresh" if fresh else "mlstm_carry",
    )(*args)
    return outs[0], outs[1:]


L_PROMPT = 256
TM_SAMPLE = 1024


def _layer_weights(l, norm_pre, norm_post, w_in_t, b_in, w_dw, b_dw, ln_g, ln_b, w_qk_conv, b_qk_conv,
                   f_bias, ml_norm, w_conv_out, w_ml_out, w_out):
    b = b_in[l]
    tail = w_in_t[l, LO_IF:, :]
    w_gate = jnp.pad(tail[:N_IF], ((0, GATE_W - N_IF), (0, 0)))[None]
    w_gc = tail[N_IF:N_IF + D_MODEL][None]
    w_gm = tail[N_IF + D_MODEL:][None]
    row = lambda a: a[None, :]
    return dict(
        norm_pre=row(norm_pre[l]), norm_post=row(norm_post[l]),
        w_pa=_pack_t(w_in_t, l, 0, D_PA), b_pa=row(b[:D_PA]),
        w_pb=_pack_t(w_in_t, l, D_PA, D_PB), b_pb=row(b[D_PA:LO_IF]),
        w_gate=_pack_t(w_gate, 0), b_gate=row(jnp.pad(b[LO_IF:LO_GC], (0, GATE_W - N_IF))),
        w_gc=_pack_t(w_gc, 0), b_gc=row(b[LO_GC:LO_GM]),
        w_gm=_pack_t(w_gm, 0), b_gm=row(b[LO_GM:]),
        w_dw=w_dw[l], b_dw=row(b_dw[l]), ln_g=row(ln_g[l]), ln_b=row(ln_b[l]),
        w_qkc=w_qk_conv[l], b_qkc=row(b_qk_conv[l]),
        f_bias=row(jnp.pad(f_bias[l], (ML_HEADS, GATE_W - N_IF))), ml_norm=row(ml_norm[l]),
        w_conv_out=_pack(w_conv_out, l), w_ml_out=_pack(w_ml_out, l), w_out=_pack(w_out, l))


def kernel(x_prompt, x_sample, state_conv, state_qk_conv, state_C, state_n, state_m, meta_tokens,
           norm_pre, norm_post, w_in, b_in, w_dw, b_dw, ln_g, ln_b, w_qk_conv, b_qk_conv, f_bias,
           ml_norm, w_conv_out, w_ml_out, w_out):
    bp, seq, _ = x_prompt.shape
    bs, dec_seq, _ = x_sample.shape
    depth = w_in.shape[0]
    L = L_PROMPT
    pad = L - N_META
    assert seq % L == 0 and pad % SUBLANES == 0 and dec_seq % (2 * SUBLANES) == 0

    meta = meta_tokens.astype(x_prompt.dtype)
    m_pad = jnp.pad(state_m, ((0, 0), (0, 0), (0, GATE_W - ML_HEADS)))[:, :, None, :]
    w_in_t = jnp.swapaxes(w_in, 1, 2)
    mstate_in = (state_qk_conv, state_C, state_n, m_pad)

    xp, xs = x_prompt, x_sample
    cb_p = cb_s = st_p = st_s = None
    for l in range(depth):
        lw = _layer_weights(l, norm_pre, norm_post, w_in_t, b_in, w_dw, b_dw, ln_g, ln_b, w_qk_conv,
                            b_qk_conv, f_bias, ml_norm, w_conv_out, w_ml_out, w_out)
        first = meta if l == 0 else None
        gcb, cb_p = _conv_branch(xp, lw, l, depth, L, meta=first, prev_out=cb_p, pad=pad)
        xp, st_p = _mlstm_branch(xp, gcb, lw, l, depth, L, meta=first, prev_out=st_p, pad=pad,
                                 drop_first=(l == depth - 1))

        xs2 = xs.reshape(bs * dec_seq, D_MODEL)
        tok = lambda a: a.reshape(bs, dec_seq, a.shape[-1])
        (pa,) = _inproj(xs2, lw["norm_pre"], lw["w_pa"], lw["b_pa"], TM_SAMPLE)
        (pgc,) = _inproj(xs2, lw["norm_pre"], lw["w_gc"], lw["b_gc"], TM_SAMPLE)
        pb, pg = _inproj(xs2, lw["norm_pre"], lw["w_pb"], lw["b_pb"], TM_SAMPLE, lw["w_gate"], lw["b_gate"])
        (pgm,) = _inproj(xs2, lw["norm_pre"], lw["w_gm"], lw["b_gm"], TM_SAMPLE)
        gcb, cb_s = _conv_branch(None, lw, l, depth, dec_seq, proj=(tok(pa), tok(pgc)),
                                 state=state_conv, prev_out=cb_s)
        xs, st_s = _mlstm_branch(xs, gcb, lw, l, depth, dec_seq, proj=(tok(pb), tok(pgm), tok(pg)),
                                 state=mstate_in, prev_out=st_s)

    fin = lambda cb, st: (cb,) + tuple(st[:3]) + (st[3][:, :, 0, :ML_HEADS],)
    return (xp, xs) + fin(cb_p, st_p) + fin(cb_s, st_s)
```

```python
import functools

import jax
import jax.numpy as jnp
from jax import lax
from jax.experimental import pallas as pl
from jax.experimental.pallas import tpu as pltpu

F32 = jnp.float32
BF16 = jnp.bfloat16
U32 = jnp.uint32

D_MODEL = 1024
D_CONV = D_MODEL
CONV_K = 31
D_ML = 2 * D_MODEL
ML_HEADS = 4
ML_HEAD_DIM = D_ML // ML_HEADS
QK_CONV_K = 4
N_META = 16
EPS = 1e-6

LANES = 128
SUBLANES = 8
GATE_W = LANES
N_IF = 2 * ML_HEADS
CONV_HIST = 32
CONV_BASE = CONV_HIST - (CONV_K - 1)
CONV_SPAN = (CONV_BASE + CONV_K - 2) // SUBLANES * SUBLANES
QK_HIST = SUBLANES
QK_BASE = QK_HIST - (QK_CONV_K - 1)
QK_CHUNK = D_ML
CONV_ROWS = 128
GLU_CHUNK = 2 * LANES
TN = 1024
VMEM_LIMIT = 62 * 1024 * 1024

D_PA = 3 * D_CONV
D_PB = 5 * D_ML
LO_IF = D_PA + D_PB
LO_GC = LO_IF + N_IF
LO_GM = LO_GC + D_MODEL
CA_A, CA_GL, CA_ZC = 0, D_CONV, 2 * D_CONV
CB_Q, CB_K, CB_V, CB_O, CB_ZM = 0, D_ML, 2 * D_ML, 3 * D_ML, 4 * D_ML


def _sigmoid(x):
    return jax.nn.sigmoid(x)


def _silu(x):
    return x * jax.nn.sigmoid(x)


def _rms(x, g):
    return x * lax.rsqrt(jnp.mean(x * x, axis=-1, keepdims=True) + EPS) * g


def _words(x_bf16):
    return pltpu.bitcast(x_bf16, U32)


def _halves(x_u32):
    return pltpu.bitcast(x_u32, BF16)


def _wdot(x_bf16, w_words):
    return jnp.dot(x_bf16, _halves(w_words), preferred_element_type=F32)


def _wdot_t(x_bf16, wt_words):
    return lax.dot_general(x_bf16, _halves(wt_words), (((1,), (1,)), ((), ())),
                           preferred_element_type=F32)


def _params(n_grid):
    return pltpu.CompilerParams(dimension_semantics=("arbitrary",) * n_grid,
                                vmem_limit_bytes=VMEM_LIMIT)


def _pack_kernel(w_ref, o_ref):
    o_ref[...] = _words(w_ref[...].astype(BF16))


def _pack(w3d, layer, col_lo=0, n_col=None):
    depth, k, n = w3d.shape
    n_col = n - col_lo if n_col is None else n_col
    tn = min(n_col, TN)
    assert col_lo % tn == 0 and n_col % tn == 0
    j0 = col_lo // tn
    return pl.pallas_call(
        _pack_kernel,
        grid=(n_col // tn,),
        in_specs=[pl.BlockSpec((k, tn), lambda j: (layer, j + j0))],
        out_specs=pl.BlockSpec((k // 2, tn), lambda j: (0, j)),
        out_shape=jax.ShapeDtypeStruct((k // 2, n_col), U32),
        compiler_params=_params(1),
        name="pack_weight",
    )(w3d.reshape(depth * k, n))


def _pack_t(wt3d, layer, row_lo=0, n_row=None):
    _, n, k = wt3d.shape
    n_row = n - row_lo if n_row is None else n_row
    tn = min(n_row, TN)
    assert row_lo % tn == 0 and n_row % tn == 0
    j0 = row_lo // tn
    return pl.pallas_call(
        _pack_kernel,
        grid=(n_row // tn,),
        in_specs=[pl.BlockSpec((None, tn, k), lambda j: (layer, j + j0, 0))],
        out_specs=pl.BlockSpec((tn // 2, k), lambda j: (j, 0)),
        out_shape=jax.ShapeDtypeStruct((n_row // 2, k), U32),
        compiler_params=_params(1),
        name="pack_weight_t",
    )(wt3d)


def _inproj_kernel(*refs, gated):
    if gated:
        x_ref, g_ref, w_ref, b_ref, wg_ref, bg_ref, o_ref, og_ref, h_ref = refs
    else:
        x_ref, g_ref, w_ref, b_ref, o_ref, h_ref = refs

    @pl.when(pl.program_id(1) == 0)
    def _():
        hb = _rms(x_ref[...], g_ref[...]).astype(BF16)
        h_ref[...] = _words(hb)
        if gated:
            og_ref[...] = _wdot_t(hb, wg_ref[...]) + bg_ref[...]

    o_ref[...] = _wdot_t(_halves(h_ref[...]), w_ref[...]) + b_ref[...]


def _inproj(x2d, g, wt_words, b, tm, w_gate=None, b_gate=None):
    n_tok, n_col = x2d.shape[0], 2 * wt_words.shape[0]
    gated = w_gate is not None
    in_specs = [
        pl.BlockSpec((tm, D_MODEL), lambda i, j: (i, 0)),
        pl.BlockSpec((1, D_MODEL), lambda i, j: (0, 0)),
        pl.BlockSpec((TN // 2, D_MODEL), lambda i, j: (j, 0)),
        pl.BlockSpec((1, TN), lambda i, j: (0, j)),
    ]
    out_specs = [pl.BlockSpec((tm, TN), lambda i, j: (i, j))]
    out_shape = [jax.ShapeDtypeStruct((n_tok, n_col), F32)]
    args = [x2d, g, wt_words, b]
    if gated:
        in_specs += [pl.BlockSpec((GATE_W // 2, D_MODEL), lambda i, j: (0, 0)),
                     pl.BlockSpec((1, GATE_W), lambda i, j: (0, 0))]
        out_specs.append(pl.BlockSpec((tm, GATE_W), lambda i, j: (i, 0)))
        out_shape.append(jax.ShapeDtypeStruct((n_tok, GATE_W), F32))
        args += [w_gate, b_gate]
    return pl.pallas_call(
        functools.partial(_inproj_kernel, gated=gated),
        grid=(n_tok // tm, n_col // TN),
        in_specs=in_specs, out_specs=out_specs, out_shape=out_shape,
        scratch_shapes=[pltpu.VMEM((tm // 2, D_MODEL), U32)],
        compiler_params=_params(2),
        name="inproj",
    )(*args)


def _cumsum_rows(x, n_rows):
    row = lax.broadcasted_iota(jnp.int32, x.shape, 0)
    k = 1
    while k < n_rows:
        x = x + jnp.where(row >= k, pltpu.roll(x, k, axis=0), 0.0)
        k *= 2
    return x


def _conv_block(proj_a, proj_gc, valid, cb_out, ubuf, ush, cvo, wdw, bdw, lng, lnb, wco, L):
    rc = min(L, CONV_ROWS)
    n_chunk = D_CONV // GLU_CHUNK
    for ck in range(n_chunk):
        co = ck * GLU_CHUNK
        u = proj_a(CA_A + co, GLU_CHUNK) * _sigmoid(proj_a(CA_GL + co, GLU_CHUNK))
        if valid is not None:
            u = jnp.where(valid, u, 0.0)
        ubuf[CONV_HIST:CONV_HIST + L, co:co + GLU_CHUNK] = u
        if ck == n_chunk - 2:
            zc = proj_a(CA_ZC, D_CONV)
        if ck == n_chunk - 1:
            gc = proj_gc(0, D_MODEL)
        for it in range(co // LANES, (co + GLU_CHUNK) // LANES):
            ls = slice(it * LANES, (it + 1) * LANES)
            sb = (it % 2) * (SUBLANES - 1)
            for sh in range(1, SUBLANES):
                ush[sb + sh - 1, :, :] = ubuf[sh:sh + L + CONV_SPAN, ls]
            for r0 in range(0, L, rc):
                acc = jnp.zeros((rc, LANES), F32) + bdw[:, ls]
                for j in range(CONV_K):
                    a8, sh = divmod(CONV_BASE + j, SUBLANES)
                    lo = a8 * SUBLANES + r0
                    src = ubuf[lo:lo + rc, ls] if sh == 0 else ush[sb + sh - 1, lo:lo + rc, :]
                    acc = acc + src * wdw[j:j + 1, ls]
                cvo[r0:r0 + rc, ls] = acc
    cb_out[0] = ubuf[L + CONV_BASE:L + CONV_HIST, :]
    ubuf[0:CONV_HIST, :] = ubuf[L:L + CONV_HIST, :]

    cv = cvo[...]
    mu = jnp.mean(cv, axis=-1, keepdims=True)
    xc = cv - mu
    ln = xc * lax.rsqrt(jnp.mean(xc * xc, axis=-1, keepdims=True) + EPS) * lng[...] + lnb[...]
    ua = _silu(ln) * _silu(zc)
    br_c = _wdot(ua.astype(BF16), wco[...])
    return _sigmoid(gc) * br_c


def _conv_scratch(L):
    return [
        pltpu.VMEM((L + CONV_HIST, D_CONV), F32),
        pltpu.VMEM((2 * (SUBLANES - 1), L + CONV_SPAN, LANES), F32),
        pltpu.VMEM((L, D_CONV), F32),
    ]


def _conv_kernel(*refs, L, pad, fresh, first_block, chained):
    refs = list(refs)
    x_ref = refs.pop(0) if fresh else None
    meta_ref = refs.pop(0) if first_block else None
    if fresh:
        npre, wpa, bpa, wgc, bgc = (refs.pop(0) for _ in range(5))
    else:
        pa_ref, pgc_ref, cb_in = (refs.pop(0) for _ in range(3))
    wdw, bdw, lng, lnb, wco = (refs.pop(0) for _ in range(5))
    if chained:
        refs.pop(0)
    gcb_ref, cb_out = refs.pop(0), refs.pop(0)
    ubuf, ush, cvo = (refs.pop(0) for _ in range(3))

    c = pl.program_id(1)

    @pl.when(c == 0)
    def _init():
        ubuf[0:CONV_HIST, :] = jnp.zeros((CONV_HIST, D_CONV), F32)
        if not fresh:
            ubuf[CONV_BASE:CONV_HIST, :] = cb_in[0]

    if fresh:
        x = x_ref[0]
        if first_block:
            x0 = jnp.concatenate([jnp.zeros((pad, D_MODEL), F32), meta_ref[...]], axis=0)
            x = jnp.where(c == 0, x0, x)
        h = _rms(x, npre[...]).astype(BF16)

        def proj_a(off, width):
            return _wdot_t(h, wpa[off // 2:(off + width) // 2, :]) + bpa[:, off:off + width]

        def proj_gc(off, width):
            return _wdot_t(h, wgc[off // 2:(off + width) // 2, :]) + bgc[:, off:off + width]
    else:
        def proj_a(off, width):
            return pa_ref[0, :, off:off + width]

        def proj_gc(off, width):
            return pgc_ref[0, :, off:off + width]

    valid = None
    if pad:
        row_id = lax.broadcasted_iota(jnp.int32, (L, 1), 0) + c * L
        valid = row_id >= pad
    gcb_ref[0] = _conv_block(proj_a, proj_gc, valid, cb_out, ubuf, ush, cvo, wdw, bdw, lng, lnb, wco, L)


def _conv_branch(x, lw, layer, depth, L, *, meta=None, proj=None, state=None, prev_out=None, pad=0):
    fresh = proj is None
    first_block = meta is not None
    chained = prev_out is not None
    src = x if fresh else proj[0]
    S = src.shape[0]
    T = src.shape[1] + (L if first_block else 0)
    shift = lambda c: jnp.maximum(c - 1, 0)
    blk = lambda s, c: (s, c, 0)
    full2 = lambda s, c: (0, 0)
    const = dict(pipeline_mode=pl.Buffered(1))
    st_shape = (depth, S, CONV_K - 1, D_CONV)
    st_spec = pl.BlockSpec((None, 1) + st_shape[2:], lambda s, c: (layer, s, 0, 0))

    in_specs, args = [], []
    if fresh:
        in_specs.append(pl.BlockSpec((1, L, D_MODEL), (lambda s, c: (s, shift(c), 0)) if first_block
                                     else blk))
        args.append(x)
        if first_block:
            in_specs.append(pl.BlockSpec(meta.shape, full2, **const))
            args.append(meta)
        ws = [lw["norm_pre"], lw["w_pa"], lw["b_pa"], lw["w_gc"], lw["b_gc"]]
    else:
        in_specs += [pl.BlockSpec((1, L, D_PA), blk), pl.BlockSpec((1, L, D_MODEL), blk), st_spec]
        args += [proj[0], proj[1], state]
        ws = []
    ws += [lw["w_dw"], lw["b_dw"], lw["ln_g"], lw["ln_b"], lw["w_conv_out"]]
    in_specs += [pl.BlockSpec(w.shape, full2, **const) for w in ws]
    args += ws
    aliases = {}
    if chained:
        aliases[len(args)] = 1
        in_specs.append(pl.BlockSpec(memory_space=pl.ANY))
        args.append(prev_out)

    return pl.pallas_call(
        functools.partial(_conv_kernel, L=L, pad=pad, fresh=fresh, first_block=first_block,
                          chained=chained),
        grid=(S, T // L),
        in_specs=in_specs,
        out_specs=[pl.BlockSpec((1, L, D_MODEL), blk), st_spec],
        out_shape=[jax.ShapeDtypeStruct((S, T, D_MODEL), F32), jax.ShapeDtypeStruct(st_shape, F32)],
        scratch_shapes=_conv_scratch(L),
        input_output_aliases=aliases,
        compiler_params=_params(2),
        name="conv_fresh" if fresh else "conv_carry",
    )(*args)


N_MSTATE = 4


def _mlstm_kernel(*refs, L, pad, fresh, first_block, chained):
    refs = list(refs)
    x_ref = refs.pop(0)
    meta_ref = refs.pop(0) if first_block else None
    if fresh:
        gcb_ref = refs.pop(0)
        npre, wpb, bpb, wgm, bgm, wgate, bgate = (refs.pop(0) for _ in range(7))
    else:
        pb_ref, pgm_ref, pg_ref, pa_ref, pgc_ref = (refs.pop(0) for _ in range(5))
        qb_in, c_in, n_in, m_in, cb_in = (refs.pop(0) for _ in range(N_MSTATE + 1))
    wqk, bqk, fb, mln, wmo, wo, npost = (refs.pop(0) for _ in range(7))
    if not fresh:
        wdw, bdw, lng, lnb, wco = (refs.pop(0) for _ in range(5))
    n_state = N_MSTATE if fresh else N_MSTATE + 1
    if chained:
        del refs[:n_state]
    y_ref, qb_out, c_out, n_out, m_out = (refs.pop(0) for _ in range(1 + N_MSTATE))
    cb_out = None if fresh else refs.pop(0)
    qkc, qkh, qks = (refs.pop(0) for _ in range(3))
    if fresh:
        cw = refs.pop(0)
    else:
        ubuf, ush, cvo = (refs.pop(0) for _ in range(3))

    c = pl.program_id(1)

    @pl.when(c == 0)
    def _init():
        qkh[...] = jnp.zeros(qkh.shape, F32)
        if fresh:
            c_out[...] = jnp.zeros(c_out.shape, F32)
            n_out[...] = jnp.zeros(n_out.shape, F32)
            m_out[...] = jnp.zeros(m_out.shape, F32)
            cw[...] = jnp.zeros(cw.shape, U32)
        else:
            qkh[QK_BASE:QK_HIST, :] = qb_in[0]
            ubuf[0:CONV_HIST, :] = jnp.zeros((CONV_HIST, D_CONV), F32)
            ubuf[CONV_BASE:CONV_HIST, :] = cb_in[0]

    if fresh:
        m_prev = m_out[0]
        c_prev = lambda h_i: c_out[0, h_i]
        c_prev_bf16 = lambda h_i: _halves(cw[h_i])
        n_prev = lambda h_i: n_out[0, h_i:h_i + 1, :]
    else:
        m_prev = m_in[0]
        c_prev = lambda h_i: c_in[0, h_i]
        c_prev_bf16 = lambda h_i: c_in[0, h_i].astype(BF16)
        n_prev = lambda h_i: n_in[0, h_i:h_i + 1, :]

    def x_block():
        x = x_ref[0]
        if first_block:
            x0 = jnp.concatenate([jnp.zeros((pad, D_MODEL), F32), meta_ref[...]], axis=0)
            x = jnp.where(c == 0, x0, x)
        return x

    if fresh:
        h = _rms(x_block(), npre[...]).astype(BF16)

        def proj(off, width):
            return _wdot_t(h, wpb[off // 2:(off + width) // 2, :]) + bpb[:, off:off + width]

        g = _wdot_t(h, wgate[...]) + bgate[...]
    else:
        def proj(off, width):
            return pb_ref[0, :, off:off + width]

        g = pg_ref[0]

    row_id = lax.broadcasted_iota(jnp.int32, (L, 1), 0) + c * L
    valid = row_id >= pad

    def qk_chunk(it):
        cs = slice(it * QK_CHUNK, (it + 1) * QK_CHUNK)
        qk_pre = proj(CB_Q + it * QK_CHUNK, QK_CHUNK)
        if pad:
            qk_pre = jnp.where(valid, qk_pre, 0.0)
        qkc[0:QK_HIST, :] = qkh[:, cs]
        qkc[QK_HIST:QK_HIST + L, :] = qk_pre
        acc = qk_pre * wqk[QK_CONV_K - 1:QK_CONV_K, cs] + bqk[:, cs]
        for j in range(QK_CONV_K - 1):
            acc = acc + qkc[QK_BASE + j:QK_BASE + j + L, :] * wqk[j:j + 1, cs]
        qkh[:, cs] = qkc[L:L + QK_HIST, :]
        acc = _silu(acc)
        if it * QK_CHUNK >= D_ML:
            acc = acc * (ML_HEAD_DIM ** -0.5)
        qks[:, cs] = _words(acc.astype(BF16))

    for it in range(2 * D_ML // QK_CHUNK):
        qk_chunk(it)
    qb_out[0] = qkh[QK_BASE:QK_HIST, :]

    logf = jax.nn.log_sigmoid(g + fb[...])
    if pad:
        logf = jnp.where(valid, logf, 0.0)
    b_al = pltpu.roll(_cumsum_rows(logf, L), GATE_W - ML_HEADS, axis=1)
    inter = b_al + m_prev
    d_src = g - b_al
    d_src_t = d_src.T
    tt = lax.broadcasted_iota(jnp.int32, (L, L), 0)
    ss = lax.broadcasted_iota(jnp.int32, (L, L), 1)
    mask = ss <= tt
    if pad:
        mask = mask & ((ss + c * L) >= pad)

    hms, updates = [], []
    for h_i in range(ML_HEADS):
        hs = slice(h_i * ML_HEAD_DIM, (h_i + 1) * ML_HEAD_DIM)
        v = proj(CB_V + h_i * ML_HEAD_DIM, ML_HEAD_DIM)
        og = proj(CB_O + h_i * ML_HEAD_DIM, ML_HEAD_DIM)
        zm = proj(CB_ZM + h_i * ML_HEAD_DIM, ML_HEAD_DIM)
        q = _halves(qks[:, hs])
        k = _halves(qks[:, D_ML + h_i * ML_HEAD_DIM:D_ML + (h_i + 1) * ML_HEAD_DIM])
        dmat = jnp.where(mask, b_al[:, h_i:h_i + 1] + d_src_t[h_i:h_i + 1, :], -jnp.inf)
        inter_h = inter[:, h_i:h_i + 1]
        m_t = jnp.maximum(inter_h, jnp.max(dmat, axis=-1, keepdims=True))
        w_intra = jnp.exp(dmat - m_t)
        w_inter = jnp.exp(inter_h - m_t)
        s_mat = lax.dot_general(q, k, (((1,), (1,)), ((), ())), preferred_element_type=F32) * w_intra
        n_old = n_prev(h_i)
        qc = lax.dot_general(q, c_prev_bf16(h_i), (((1,), (1,)), ((), ())), preferred_element_type=F32)
        num = jnp.dot(s_mat.astype(BF16), v.astype(BF16), preferred_element_type=F32) + w_inter * qc
        qn = jnp.sum(q.astype(F32) * n_old, axis=-1, keepdims=True)
        den = jnp.sum(s_mat, axis=-1, keepdims=True) + w_inter * qn
        hh = num * (1.0 / jnp.maximum(jnp.abs(den), jnp.exp(-m_t)))
        hh = hh * lax.rsqrt(jnp.mean(hh * hh, axis=-1, keepdims=True) + EPS) * mln[:, hs]
        hm = hh * _sigmoid(og) * _silu(zm)
        hms.append(hm.astype(BF16))

        m_new = m_t[L - 1:L, :]
        decay = jnp.exp(b_al[L - 1:L, h_i:h_i + 1] + m_prev[:, h_i:h_i + 1] - m_new)
        w_s = jnp.exp(d_src[:, h_i:h_i + 1] + (b_al[L - 1:L, h_i:h_i + 1] - m_new))
        if pad:
            w_s = jnp.where(valid, w_s, 0.0)
        vw = (w_s * v).astype(BF16)
        n_new = decay * n_old + jnp.sum(w_s * k.astype(F32), axis=0, keepdims=True)
        updates.append((m_new, decay, vw, k, n_new))

    if fresh:
        gm = _wdot_t(h, wgm[...]) + bgm[...]
    else:
        gm = pgm_ref[0]
    br_m = _wdot(jnp.concatenate(hms, axis=1), wmo[...])
    if fresh:
        gcb = gcb_ref[0]
    else:
        gcb = _conv_block(lambda off, width: pa_ref[0, :, off:off + width],
                          lambda off, width: pgc_ref[0, :, off:off + width],
                          None, cb_out, ubuf, ush, cvo, wdw, bdw, lng, lnb, wco, L)
    ym = gcb + _sigmoid(gm) * br_m
    out = _wdot(ym.astype(BF16), wo[...])
    y_ref[0] = x_block() + _rms(out, npost[...])

    lane = lax.broadcasted_iota(jnp.int32, (1, GATE_W), 1)
    m_row = jnp.zeros((1, GATE_W), F32)
    for h_i, (m_new, decay, vw, k, n_new) in enumerate(updates):
        c_new = decay * c_prev(h_i) + lax.dot_general(
            vw, k, (((0,), (0,)), ((), ())), preferred_element_type=F32)
        c_out[0, h_i] = c_new
        if fresh:
            cw[h_i] = _words(c_new.astype(BF16))
        n_out[0, h_i:h_i + 1, :] = n_new
        m_row = jnp.where(lane == h_i, m_new, m_row)
    m_out[0] = m_row


def _mstate_shapes(depth, S):
    return [
        (depth, S, QK_CONV_K - 1, 2 * D_ML),
        (depth, S, ML_HEADS, ML_HEAD_DIM, ML_HEAD_DIM),
        (depth, S, ML_HEADS, ML_HEAD_DIM),
        (depth, S, 1, GATE_W),
    ]


def _mlstm_branch(x, gcb, lw, layer, depth, L, *, meta=None, proj=None, state=None, prev_out=None,
                  pad=0, drop_first=False):
    fresh = proj is None
    first_block = meta is not None
    chained = prev_out is not None
    S = x.shape[0]
    T = gcb.shape[1] if fresh else x.shape[1]
    assert fresh or T == L
    shapes = _mstate_shapes(depth, S)
    if not fresh:
        shapes.append((depth, S, CONV_K - 1, D_CONV))
    shift = lambda c: jnp.maximum(c - 1, 0)
    blk = lambda s, c: (s, c, 0)
    full2 = lambda s, c: (0, 0)
    const = dict(pipeline_mode=pl.Buffered(1))

    def state_spec(shape, **kw):
        nd = len(shape) - 2
        return pl.BlockSpec((None, 1) + shape[2:], lambda s, c: (layer, s) + (0,) * nd, **kw)

    in_specs = [pl.BlockSpec((1, L, D_MODEL), (lambda s, c: (s, shift(c), 0)) if first_block else blk)]
    args = [x]
    if first_block:
        in_specs.append(pl.BlockSpec(meta.shape, full2, **const))
        args.append(meta)
    if fresh:
        in_specs.append(pl.BlockSpec((1, L, D_MODEL), blk))
        args.append(gcb)
        ws = [lw["norm_pre"], lw["w_pb"], lw["b_pb"], lw["w_gm"], lw["b_gm"], lw["w_gate"], lw["b_gate"]]
    else:
        in_specs += [pl.BlockSpec((1, L, p.shape[-1]), blk) for p in proj]
        args += list(proj)
        in_specs += [state_spec(sh) for sh in shapes]
        args += list(state)
        ws = []
    ws += [lw["w_qkc"], lw["b_qkc"], lw["f_bias"], lw["ml_norm"], lw["w_ml_out"], lw["w_out"],
           lw["norm_post"]]
    if not fresh:
        ws += [lw["w_dw"], lw["b_dw"], lw["ln_g"], lw["ln_b"], lw["w_conv_out"]]
    in_specs += [pl.BlockSpec(w.shape, full2, **const) for w in ws]
    args += ws
    aliases = {}
    if chained:
        for k, a in enumerate(prev_out):
            aliases[len(args)] = 1 + k
            in_specs.append(pl.BlockSpec(memory_space=pl.ANY))
            args.append(a)

    t_out = T - L if drop_first else T
    out_shape = [jax.ShapeDtypeStruct((S, t_out, D_MODEL), F32)] + \
        [jax.ShapeDtypeStruct(sh, F32) for sh in shapes]
    out_specs = [pl.BlockSpec((1, L, D_MODEL), (lambda s, c: (s, shift(c), 0)) if drop_first else blk)] + \
        [state_spec(sh) for sh in shapes]
    scratch = [
        pltpu.VMEM((L + QK_HIST, QK_CHUNK), F32),
        pltpu.VMEM((QK_HIST, 2 * D_ML), F32),
        pltpu.VMEM((L // 2, 2 * D_ML), U32),
    ]
    if fresh:
        scratch.append(pltpu.VMEM((ML_HEADS, ML_HEAD_DIM // 2, ML_HEAD_DIM), U32))
    else:
        scratch += _conv_scratch(L)
    outs = pl.pallas_call(
        functools.partial(_mlstm_kernel, L=L, pad=pad, fresh=fresh, first_block=first_block,
                          chained=chained),
        grid=(S, T // L),
        in_specs=in_specs,
        out_specs=out_specs,
        out_shape=out_shape,
        scratch_shapes=scratch,
        input_output_aliases=aliases,
        compiler_params=_params(2),
        name="mlstm_fresh" if fresh else "mlstm_carry",
    )(*args)
    return outs[0], outs[1:]


L_PROMPT = 256
TM_SAMPLE = 1024


def _layer_weights(l, norm_pre, norm_post, w_in_t, b_in, w_dw, b_dw, ln_g, ln_b, w_qk_conv, b_qk_conv,
                   f_bias, ml_norm, w_conv_out, w_ml_out, w_out):
    b = b_in[l]
    tail = w_in_t[l, LO_IF:, :]
    w_gate = jnp.pad(tail[:N_IF], ((0, GATE_W - N_IF), (0, 0)))[None]
    w_gc = tail[N_IF:N_IF + D_MODEL][None]
    w_gm = tail[N_IF + D_MODEL:][None]
    row = lambda a: a[None, :]
    return dict(
        norm_pre=row(norm_pre[l]), norm_post=row(norm_post[l]),
        w_pa=_pack_t(w_in_t, l, 0, D_PA), b_pa=row(b[:D_PA]),
        w_pb=_pack_t(w_in_t, l, D_PA, D_PB), b_pb=row(b[D_PA:LO_IF]),
        w_gate=_pack_t(w_gate, 0), b_gate=row(jnp.pad(b[LO_IF:LO_GC], (0, GATE_W - N_IF))),
        w_gc=_pack_t(w_gc, 0), b_gc=row(b[LO_GC:LO_GM]),
        w_gm=_pack_t(w_gm, 0), b_gm=row(b[LO_GM:]),
        w_dw=w_dw[l], b_dw=row(b_dw[l]), ln_g=row(ln_g[l]), ln_b=row(ln_b[l]),
        w_qkc=w_qk_conv[l], b_qkc=row(b_qk_conv[l]),
        f_bias=row(jnp.pad(f_bias[l], (ML_HEADS, GATE_W - N_IF))), ml_norm=row(ml_norm[l]),
        w_conv_out=_pack(w_conv_out, l), w_ml_out=_pack(w_ml_out, l), w_out=_pack(w_out, l))


def kernel(x_prompt, x_sample, state_conv, state_qk_conv, state_C, state_n, state_m, meta_tokens,
           norm_pre, norm_post, w_in, b_in, w_dw, b_dw, ln_g, ln_b, w_qk_conv, b_qk_conv, f_bias,
           ml_norm, w_conv_out, w_ml_out, w_out):
    bp, seq, _ = x_prompt.shape
    bs, dec_seq, _ = x_sample.shape
    depth = w_in.shape[0]
    L = L_PROMPT
    pad = L - N_META
    assert seq % L == 0 and pad % SUBLANES == 0 and dec_seq % (2 * SUBLANES) == 0

    meta = meta_tokens.astype(x_prompt.dtype)
    m_pad = jnp.pad(state_m, ((0, 0), (0, 0), (0, GATE_W - ML_HEADS)))[:, :, None, :]
    w_in_t = jnp.swapaxes(w_in, 1, 2)
    mstate_in = (state_qk_conv, state_C, state_n, m_pad)

    xp, xs = x_prompt, x_sample
    cb_p = st_p = st_s = None
    for l in range(depth):
        lw = _layer_weights(l, norm_pre, norm_post, w_in_t, b_in, w_dw, b_dw, ln_g, ln_b, w_qk_conv,
                            b_qk_conv, f_bias, ml_norm, w_conv_out, w_ml_out, w_out)
        first = meta if l == 0 else None
        gcb, cb_p = _conv_branch(xp, lw, l, depth, L, meta=first, prev_out=cb_p, pad=pad)
        xp, st_p = _mlstm_branch(xp, gcb, lw, l, depth, L, meta=first, prev_out=st_p, pad=pad,
                                 drop_first=(l == depth - 1))

        xs2 = xs.reshape(bs * dec_seq, D_MODEL)
        tok = lambda a: a.reshape(bs, dec_seq, a.shape[-1])
        (pa,) = _inproj(xs2, lw["norm_pre"], lw["w_pa"], lw["b_pa"], TM_SAMPLE)
        (pgc,) = _inproj(xs2, lw["norm_pre"], lw["w_gc"], lw["b_gc"], TM_SAMPLE)
        pb, pg = _inproj(xs2, lw["norm_pre"], lw["w_pb"], lw["b_pb"], TM_SAMPLE, lw["w_gate"], lw["b_gate"])
        (pgm,) = _inproj(xs2, lw["norm_pre"], lw["w_gm"], lw["b_gm"], TM_SAMPLE)
        xs, st_s = _mlstm_branch(xs, None, lw, l, depth, dec_seq,
                                 proj=(tok(pb), tok(pgm), tok(pg), tok(pa), tok(pgc)),
                                 state=mstate_in + (state_conv,), prev_out=st_s)

    fin = lambda cb, st: (cb,) + tuple(st[:3]) + (st[3][:, :, 0, :ML_HEADS],)
    return (xp, xs) + fin(cb_p, st_p) + fin(st_s[4], st_s)
```

```python
import functools

import jax
import jax.numpy as jnp
from jax import lax
from jax.experimental import pallas as pl
from jax.experimental.pallas import tpu as pltpu

F32 = jnp.float32
BF16 = jnp.bfloat16
U32 = jnp.uint32

D_MODEL = 1024
D_CONV = D_MODEL
CONV_K = 31
D_ML = 2 * D_MODEL
ML_HEADS = 4
ML_HEAD_DIM = D_ML // ML_HEADS
QK_CONV_K = 4
N_META = 16
EPS = 1e-6

LANES = 128
SUBLANES = 8
GATE_W = LANES
N_IF = 2 * ML_HEADS
CONV_HIST = 32
CONV_BASE = CONV_HIST - (CONV_K - 1)
CONV_SPAN = (CONV_BASE + CONV_K - 2) // SUBLANES * SUBLANES
QK_HIST = SUBLANES
QK_BASE = QK_HIST - (QK_CONV_K - 1)
QK_CHUNK = D_ML
CONV_ROWS = 128
GLU_CHUNK = 2 * LANES
TN = 1024
VMEM_LIMIT = 62 * 1024 * 1024

D_PA = 3 * D_CONV
D_PB = 5 * D_ML
LO_IF = D_PA + D_PB
LO_GC = LO_IF + N_IF
LO_GM = LO_GC + D_MODEL
CA_A, CA_GL, CA_ZC = 0, D_CONV, 2 * D_CONV
CB_Q, CB_K, CB_V, CB_O, CB_ZM = 0, D_ML, 2 * D_ML, 3 * D_ML, 4 * D_ML


def _sigmoid(x):
    return jax.nn.sigmoid(x)


def _silu(x):
    return x * jax.nn.sigmoid(x)


def _rms(x, g):
    return x * lax.rsqrt(jnp.mean(x * x, axis=-1, keepdims=True) + EPS) * g


def _words(x_bf16):
    return pltpu.bitcast(x_bf16, U32)


def _halves(x_u32):
    return pltpu.bitcast(x_u32, BF16)


def _wdot(x_bf16, w_words):
    return jnp.dot(x_bf16, _halves(w_words), preferred_element_type=F32)


def _wdot_t(x_bf16, wt_words):
    return lax.dot_general(x_bf16, _halves(wt_words), (((1,), (1,)), ((), ())),
                           preferred_element_type=F32)


def _params(n_grid):
    return pltpu.CompilerParams(dimension_semantics=("arbitrary",) * n_grid,
                                vmem_limit_bytes=VMEM_LIMIT)


def _pack_kernel(w_ref, o_ref):
    o_ref[...] = _words(w_ref[...].astype(BF16))


def _pack(w3d, layer, col_lo=0, n_col=None):
    depth, k, n = w3d.shape
    n_col = n - col_lo if n_col is None else n_col
    tn = min(n_col, TN)
    assert col_lo % tn == 0 and n_col % tn == 0
    j0 = col_lo // tn
    return pl.pallas_call(
        _pack_kernel,
        grid=(n_col // tn,),
        in_specs=[pl.BlockSpec((k, tn), lambda j: (layer, j + j0))],
        out_specs=pl.BlockSpec((k // 2, tn), lambda j: (0, j)),
        out_shape=jax.ShapeDtypeStruct((k // 2, n_col), U32),
        compiler_params=_params(1),
        name="pack_weight",
    )(w3d.reshape(depth * k, n))


def _pack_t(wt3d, layer, row_lo=0, n_row=None):
    _, n, k = wt3d.shape
    n_row = n - row_lo if n_row is None else n_row
    tn = min(n_row, TN)
    assert row_lo % tn == 0 and n_row % tn == 0
    j0 = row_lo // tn
    return pl.pallas_call(
        _pack_kernel,
        grid=(n_row // tn,),
        in_specs=[pl.BlockSpec((None, tn, k), lambda j: (layer, j + j0, 0))],
        out_specs=pl.BlockSpec((tn // 2, k), lambda j: (j, 0)),
        out_shape=jax.ShapeDtypeStruct((n_row // 2, k), U32),
        compiler_params=_params(1),
        name="pack_weight_t",
    )(wt3d)


def _inproj_kernel(*refs, gated):
    if gated:
        x_ref, g_ref, w_ref, b_ref, wg_ref, bg_ref, o_ref, og_ref, h_ref = refs
    else:
        x_ref, g_ref, w_ref, b_ref, o_ref, h_ref = refs

    @pl.when(pl.program_id(1) == 0)
    def _():
        hb = _rms(x_ref[...], g_ref[...]).astype(BF16)
        h_ref[...] = _words(hb)
        if gated:
            og_ref[...] = _wdot_t(hb, wg_ref[...]) + bg_ref[...]

    o_ref[...] = _wdot_t(_halves(h_ref[...]), w_ref[...]) + b_ref[...]


def _inproj(x2d, g, wt_words, b, tm, w_gate=None, b_gate=None):
    n_tok, n_col = x2d.shape[0], 2 * wt_words.shape[0]
    gated = w_gate is not None
    in_specs = [
        pl.BlockSpec((tm, D_MODEL), lambda i, j: (i, 0)),
        pl.BlockSpec((1, D_MODEL), lambda i, j: (0, 0)),
        pl.BlockSpec((TN // 2, D_MODEL), lambda i, j: (j, 0)),
        pl.BlockSpec((1, TN), lambda i, j: (0, j)),
    ]
    out_specs = [pl.BlockSpec((tm, TN), lambda i, j: (i, j))]
    out_shape = [jax.ShapeDtypeStruct((n_tok, n_col), F32)]
    args = [x2d, g, wt_words, b]
    if gated:
        in_specs += [pl.BlockSpec((GATE_W // 2, D_MODEL), lambda i, j: (0, 0)),
                     pl.BlockSpec((1, GATE_W), lambda i, j: (0, 0))]
        out_specs.append(pl.BlockSpec((tm, GATE_W), lambda i, j: (i, 0)))
        out_shape.append(jax.ShapeDtypeStruct((n_tok, GATE_W), F32))
        args += [w_gate, b_gate]
    return pl.pallas_call(
        functools.partial(_inproj_kernel, gated=gated),
        grid=(n_tok // tm, n_col // TN),
        in_specs=in_specs, out_specs=out_specs, out_shape=out_shape,
        scratch_shapes=[pltpu.VMEM((tm // 2, D_MODEL), U32)],
        compiler_params=_params(2),
        name="inproj",
    )(*args)


def _cumsum_rows(x, n_rows):
    row = lax.broadcasted_iota(jnp.int32, x.shape, 0)
    k = 1
    while k < n_rows:
        x = x + jnp.where(row >= k, pltpu.roll(x, k, axis=0), 0.0)
        k *= 2
    return x


def _conv_block(proj_a, proj_gc, valid, cb_out, ubuf, ush, cvo, wdw, bdw, lng, lnb, wco, L):
    rc = min(L, CONV_ROWS)
    n_chunk = D_CONV // GLU_CHUNK
    for ck in range(n_chunk):
        co = ck * GLU_CHUNK
        u = proj_a(CA_A + co, GLU_CHUNK) * _sigmoid(proj_a(CA_GL + co, GLU_CHUNK))
        if valid is not None:
            u = jnp.where(valid, u, 0.0)
        ubuf[CONV_HIST:CONV_HIST + L, co:co + GLU_CHUNK] = u
        if ck == n_chunk - 2:
            zc = proj_a(CA_ZC, D_CONV)
        if ck == n_chunk - 1:
            gc = proj_gc(0, D_MODEL)
        for it in range(co // LANES, (co + GLU_CHUNK) // LANES):
            ls = slice(it * LANES, (it + 1) * LANES)
            sb = (it % 2) * (SUBLANES - 1)
            for sh in range(1, SUBLANES):
                ush[sb + sh - 1, :, :] = ubuf[sh:sh + L + CONV_SPAN, ls]
            for r0 in range(0, L, rc):
                acc = jnp.zeros((rc, LANES), F32) + bdw[:, ls]
                for j in range(CONV_K):
                    a8, sh = divmod(CONV_BASE + j, SUBLANES)
                    lo = a8 * SUBLANES + r0
                    src = ubuf[lo:lo + rc, ls] if sh == 0 else ush[sb + sh - 1, lo:lo + rc, :]
                    acc = acc + src * wdw[j:j + 1, ls]
                cvo[r0:r0 + rc, ls] = acc
    cb_out[0] = ubuf[L + CONV_BASE:L + CONV_HIST, :]
    ubuf[0:CONV_HIST, :] = ubuf[L:L + CONV_HIST, :]

    cv = cvo[...]
    mu = jnp.mean(cv, axis=-1, keepdims=True)
    xc = cv - mu
    ln = xc * lax.rsqrt(jnp.mean(xc * xc, axis=-1, keepdims=True) + EPS) * lng[...] + lnb[...]
    ua = _silu(ln) * _silu(zc)
    br_c = _wdot(ua.astype(BF16), wco[...])
    return _sigmoid(gc) * br_c


def _conv_scratch(L):
    return [
        pltpu.VMEM((L + CONV_HIST, D_CONV), F32),
        pltpu.VMEM((2 * (SUBLANES - 1), L + CONV_SPAN, LANES), F32),
        pltpu.VMEM((L, D_CONV), F32),
    ]


def _conv_kernel(*refs, L, pad, fresh, first_block, chained):
    refs = list(refs)
    x_ref = refs.pop(0) if fresh else None
    meta_ref = refs.pop(0) if first_block else None
    if fresh:
        npre, wpa, bpa, wgc, bgc = (refs.pop(0) for _ in range(5))
    else:
        pa_ref, pgc_ref, cb_in = (refs.pop(0) for _ in range(3))
    wdw, bdw, lng, lnb, wco = (refs.pop(0) for _ in range(5))
    if chained:
        refs.pop(0)
    gcb_ref, cb_out = refs.pop(0), refs.pop(0)
    ubuf, ush, cvo = (refs.pop(0) for _ in range(3))

    c = pl.program_id(1)

    @pl.when(c == 0)
    def _init():
        ubuf[0:CONV_HIST, :] = jnp.zeros((CONV_HIST, D_CONV), F32)
        if not fresh:
            ubuf[CONV_BASE:CONV_HIST, :] = cb_in[0]

    if fresh:
        x = x_ref[0]
        if first_block:
            x0 = jnp.concatenate([jnp.zeros((pad, D_MODEL), F32), meta_ref[...]], axis=0)
            x = jnp.where(c == 0, x0, x)
        h = _rms(x, npre[...]).astype(BF16)

        def proj_a(off, width):
            return _wdot_t(h, wpa[off // 2:(off + width) // 2, :]) + bpa[:, off:off + width]

        def proj_gc(off, width):
            return _wdot_t(h, wgc[off // 2:(off + width) // 2, :]) + bgc[:, off:off + width]
    else:
        def proj_a(off, width):
            return pa_ref[0, :, off:off + width]

        def proj_gc(off, width):
            return pgc_ref[0, :, off:off + width]

    valid = None
    if pad:
        row_id = lax.broadcasted_iota(jnp.int32, (L, 1), 0) + c * L
        valid = row_id >= pad
    gcb_ref[0] = _conv_block(proj_a, proj_gc, valid, cb_out, ubuf, ush, cvo, wdw, bdw, lng, lnb, wco, L)


def _conv_branch(x, lw, layer, depth, L, *, meta=None, proj=None, state=None, prev_out=None, pad=0):
    fresh = proj is None
    first_block = meta is not None
    chained = prev_out is not None
    src = x if fresh else proj[0]
    S = src.shape[0]
    T = src.shape[1] + (L if first_block else 0)
    shift = lambda c: jnp.maximum(c - 1, 0)
    blk = lambda s, c: (s, c, 0)
    full2 = lambda s, c: (0, 0)
    const = dict(pipeline_mode=pl.Buffered(1))
    st_shape = (depth, S, CONV_K - 1, D_CONV)
    st_spec = pl.BlockSpec((None, 1) + st_shape[2:], lambda s, c: (layer, s, 0, 0))

    in_specs, args = [], []
    if fresh:
        in_specs.append(pl.BlockSpec((1, L, D_MODEL), (lambda s, c: (s, shift(c), 0)) if first_block
                                     else blk))
        args.append(x)
        if first_block:
            in_specs.append(pl.BlockSpec(meta.shape, full2, **const))
            args.append(meta)
        ws = [lw["norm_pre"], lw["w_pa"], lw["b_pa"], lw["w_gc"], lw["b_gc"]]
    else:
        in_specs += [pl.BlockSpec((1, L, D_PA), blk), pl.BlockSpec((1, L, D_MODEL), blk), st_spec]
        args += [proj[0], proj[1], state]
        ws = []
    ws += [lw["w_dw"], lw["b_dw"], lw["ln_g"], lw["ln_b"], lw["w_conv_out"]]
    in_specs += [pl.BlockSpec(w.shape, full2, **const) for w in ws]
    args += ws
    aliases = {}
    if chained:
        aliases[len(args)] = 1
        in_specs.append(pl.BlockSpec(memory_space=pl.ANY))
        args.append(prev_out)

    return pl.pallas_call(
        functools.partial(_conv_kernel, L=L, pad=pad, fresh=fresh, first_block=first_block,
                          chained=chained),
        grid=(S, T // L),
        in_specs=in_specs,
        out_specs=[pl.BlockSpec((1, L, D_MODEL), blk), st_spec],
        out_shape=[jax.ShapeDtypeStruct((S, T, D_MODEL), F32), jax.ShapeDtypeStruct(st_shape, F32)],
        scratch_shapes=_conv_scratch(L),
        input_output_aliases=aliases,
        compiler_params=_params(2),
        name="conv_fresh" if fresh else "conv_carry",
    )(*args)


N_MSTATE = 4


def _mlstm_kernel(*refs, L, pad, fresh, first_block, chained):
    refs = list(refs)
    x_ref = refs.pop(0)
    meta_ref = refs.pop(0) if first_block else None
    if fresh:
        gcb_ref = refs.pop(0)
        npre, wpb, bpb, wgm, bgm, wgate, bgate = (refs.pop(0) for _ in range(7))
    else:
        pb_ref, pg_ref, pa_ref = (refs.pop(0) for _ in range(3))
        qb_in, c_in, n_in, m_in, cb_in = (refs.pop(0) for _ in range(N_MSTATE + 1))
    wqk, bqk, fb, mln, wmo, wo, npost = (refs.pop(0) for _ in range(7))
    if not fresh:
        wdw, bdw, lng, lnb, wco = (refs.pop(0) for _ in range(5))
    n_state = N_MSTATE if fresh else N_MSTATE + 1
    if chained:
        del refs[:n_state]
    y_ref, qb_out, c_out, n_out, m_out = (refs.pop(0) for _ in range(1 + N_MSTATE))
    cb_out = None if fresh else refs.pop(0)
    qkc, qkh, qks = (refs.pop(0) for _ in range(3))
    if fresh:
        cw = refs.pop(0)
    else:
        ubuf, ush, cvo = (refs.pop(0) for _ in range(3))

    c = pl.program_id(1)

    @pl.when(c == 0)
    def _init():
        qkh[...] = jnp.zeros(qkh.shape, F32)
        if fresh:
            c_out[...] = jnp.zeros(c_out.shape, F32)
            n_out[...] = jnp.zeros(n_out.shape, F32)
            m_out[...] = jnp.zeros(m_out.shape, F32)
            cw[...] = jnp.zeros(cw.shape, U32)
        else:
            qkh[QK_BASE:QK_HIST, :] = qb_in[0]
            ubuf[0:CONV_HIST, :] = jnp.zeros((CONV_HIST, D_CONV), F32)
            ubuf[CONV_BASE:CONV_HIST, :] = cb_in[0]

    if fresh:
        m_prev = m_out[0]
        c_prev = lambda h_i: c_out[0, h_i]
        c_prev_bf16 = lambda h_i: _halves(cw[h_i])
        n_prev = lambda h_i: n_out[0, h_i:h_i + 1, :]
    else:
        m_prev = m_in[0]
        c_prev = lambda h_i: c_in[0, h_i]
        c_prev_bf16 = lambda h_i: c_in[0, h_i].astype(BF16)
        n_prev = lambda h_i: n_in[0, h_i:h_i + 1, :]

    def x_block():
        x = x_ref[0]
        if first_block:
            x0 = jnp.concatenate([jnp.zeros((pad, D_MODEL), F32), meta_ref[...]], axis=0)
            x = jnp.where(c == 0, x0, x)
        return x

    if fresh:
        h = _rms(x_block(), npre[...]).astype(BF16)

        def proj(off, width):
            return _wdot_t(h, wpb[off // 2:(off + width) // 2, :]) + bpb[:, off:off + width]

        g = _wdot_t(h, wgate[...]) + bgate[...]
    else:
        def proj(off, width):
            return pb_ref[0, :, off:off + width]

        g = pg_ref[0]

    row_id = lax.broadcasted_iota(jnp.int32, (L, 1), 0) + c * L
    valid = row_id >= pad

    def qk_chunk(it):
        cs = slice(it * QK_CHUNK, (it + 1) * QK_CHUNK)
        qk_pre = proj(CB_Q + it * QK_CHUNK, QK_CHUNK)
        if pad:
            qk_pre = jnp.where(valid, qk_pre, 0.0)
        qkc[0:QK_HIST, :] = qkh[:, cs]
        qkc[QK_HIST:QK_HIST + L, :] = qk_pre
        acc = qk_pre * wqk[QK_CONV_K - 1:QK_CONV_K, cs] + bqk[:, cs]
        for j in range(QK_CONV_K - 1):
            acc = acc + qkc[QK_BASE + j:QK_BASE + j + L, :] * wqk[j:j + 1, cs]
        qkh[:, cs] = qkc[L:L + QK_HIST, :]
        acc = _silu(acc)
        if it * QK_CHUNK >= D_ML:
            acc = acc * (ML_HEAD_DIM ** -0.5)
        qks[:, cs] = _words(acc.astype(BF16))

    for it in range(2 * D_ML // QK_CHUNK):
        qk_chunk(it)
    qb_out[0] = qkh[QK_BASE:QK_HIST, :]

    logf = jax.nn.log_sigmoid(g + fb[...])
    if pad:
        logf = jnp.where(valid, logf, 0.0)
    b_al = pltpu.roll(_cumsum_rows(logf, L), GATE_W - ML_HEADS, axis=1)
    inter = b_al + m_prev
    d_src = g - b_al
    d_src_t = d_src.T
    tt = lax.broadcasted_iota(jnp.int32, (L, L), 0)
    ss = lax.broadcasted_iota(jnp.int32, (L, L), 1)
    mask = ss <= tt
    if pad:
        mask = mask & ((ss + c * L) >= pad)

    hms, updates = [], []
    for h_i in range(ML_HEADS):
        hs = slice(h_i * ML_HEAD_DIM, (h_i + 1) * ML_HEAD_DIM)
        v = proj(CB_V + h_i * ML_HEAD_DIM, ML_HEAD_DIM)
        og = proj(CB_O + h_i * ML_HEAD_DIM, ML_HEAD_DIM)
        zm = proj(CB_ZM + h_i * ML_HEAD_DIM, ML_HEAD_DIM)
        q = _halves(qks[:, hs])
        k = _halves(qks[:, D_ML + h_i * ML_HEAD_DIM:D_ML + (h_i + 1) * ML_HEAD_DIM])
        dmat = jnp.where(mask, b_al[:, h_i:h_i + 1] + d_src_t[h_i:h_i + 1, :], -jnp.inf)
        inter_h = inter[:, h_i:h_i + 1]
        m_t = jnp.maximum(inter_h, jnp.max(dmat, axis=-1, keepdims=True))
        w_intra = jnp.exp(dmat - m_t)
        w_inter = jnp.exp(inter_h - m_t)
        s_mat = lax.dot_general(q, k, (((1,), (1,)), ((), ())), preferred_element_type=F32) * w_intra
        n_old = n_prev(h_i)
        qc = lax.dot_general(q, c_prev_bf16(h_i), (((1,), (1,)), ((), ())), preferred_element_type=F32)
        num = jnp.dot(s_mat.astype(BF16), v.astype(BF16), preferred_element_type=F32) + w_inter * qc
        qn = jnp.sum(q.astype(F32) * n_old, axis=-1, keepdims=True)
        den = jnp.sum(s_mat, axis=-1, keepdims=True) + w_inter * qn
        hh = num * (1.0 / jnp.maximum(jnp.abs(den), jnp.exp(-m_t)))
        hh = hh * lax.rsqrt(jnp.mean(hh * hh, axis=-1, keepdims=True) + EPS) * mln[:, hs]
        hm = hh * _sigmoid(og) * _silu(zm)
        hms.append(hm.astype(BF16))

        m_new = m_t[L - 1:L, :]
        decay = jnp.exp(b_al[L - 1:L, h_i:h_i + 1] + m_prev[:, h_i:h_i + 1] - m_new)
        w_s = jnp.exp(d_src[:, h_i:h_i + 1] + (b_al[L - 1:L, h_i:h_i + 1] - m_new))
        if pad:
            w_s = jnp.where(valid, w_s, 0.0)
        vw = (w_s * v).astype(BF16)
        n_new = decay * n_old + jnp.sum(w_s * k.astype(F32), axis=0, keepdims=True)
        updates.append((m_new, decay, vw, k, n_new))

    if fresh:
        gm = _wdot_t(h, wgm[...]) + bgm[...]
    else:
        gm = pb_ref[0, :, D_PB:D_PB + D_MODEL]
    br_m = _wdot(jnp.concatenate(hms, axis=1), wmo[...])
    if fresh:
        gcb = gcb_ref[0]
    else:
        gcb = _conv_block(lambda off, width: pa_ref[0, :, off:off + width],
                          lambda off, width: pa_ref[0, :, D_PA + off:D_PA + off + width],
                          None, cb_out, ubuf, ush, cvo, wdw, bdw, lng, lnb, wco, L)
    ym = gcb + _sigmoid(gm) * br_m
    out = _wdot(ym.astype(BF16), wo[...])
    y_ref[0] = x_block() + _rms(out, npost[...])

    lane = lax.broadcasted_iota(jnp.int32, (1, GATE_W), 1)
    m_row = jnp.zeros((1, GATE_W), F32)
    for h_i, (m_new, decay, vw, k, n_new) in enumerate(updates):
        c_new = decay * c_prev(h_i) + lax.dot_general(
            vw, k, (((0,), (0,)), ((), ())), preferred_element_type=F32)
        c_out[0, h_i] = c_new
        if fresh:
            cw[h_i] = _words(c_new.astype(BF16))
        n_out[0, h_i:h_i + 1, :] = n_new
        m_row = jnp.where(lane == h_i, m_new, m_row)
    m_out[0] = m_row


def _mstate_shapes(depth, S):
    return [
        (depth, S, QK_CONV_K - 1, 2 * D_ML),
        (depth, S, ML_HEADS, ML_HEAD_DIM, ML_HEAD_DIM),
        (depth, S, ML_HEADS, ML_HEAD_DIM),
        (depth, S, 1, GATE_W),
    ]


def _mlstm_branch(x, gcb, lw, layer, depth, L, *, meta=None, proj=None, state=None, prev_out=None,
                  pad=0, drop_first=False):
    fresh = proj is None
    first_block = meta is not None
    chained = prev_out is not None
    S = x.shape[0]
    T = gcb.shape[1] if fresh else x.shape[1]
    assert fresh or T == L
    shapes = _mstate_shapes(depth, S)
    if not fresh:
        shapes.append((depth, S, CONV_K - 1, D_CONV))
    shift = lambda c: jnp.maximum(c - 1, 0)
    blk = lambda s, c: (s, c, 0)
    full2 = lambda s, c: (0, 0)
    const = dict(pipeline_mode=pl.Buffered(1))

    def state_spec(shape, **kw):
        nd = len(shape) - 2
        return pl.BlockSpec((None, 1) + shape[2:], lambda s, c: (layer, s) + (0,) * nd, **kw)

    in_specs = [pl.BlockSpec((1, L, D_MODEL), (lambda s, c: (s, shift(c), 0)) if first_block else blk)]
    args = [x]
    if first_block:
        in_specs.append(pl.BlockSpec(meta.shape, full2, **const))
        args.append(meta)
    if fresh:
        in_specs.append(pl.BlockSpec((1, L, D_MODEL), blk))
        args.append(gcb)
        ws = [lw["norm_pre"], lw["w_pb"], lw["b_pb"], lw["w_gm"], lw["b_gm"], lw["w_gate"], lw["b_gate"]]
    else:
        in_specs += [pl.BlockSpec((1, L, p.shape[-1]), blk) for p in proj]
        args += list(proj)
        in_specs += [state_spec(sh) for sh in shapes]
        args += list(state)
        ws = []
    ws += [lw["w_qkc"], lw["b_qkc"], lw["f_bias"], lw["ml_norm"], lw["w_ml_out"], lw["w_out"],
           lw["norm_post"]]
    if not fresh:
        ws += [lw["w_dw"], lw["b_dw"], lw["ln_g"], lw["ln_b"], lw["w_conv_out"]]
    in_specs += [pl.BlockSpec(w.shape, full2, **const) for w in ws]
    args += ws
    aliases = {}
    if chained:
        for k, a in enumerate(prev_out):
            aliases[len(args)] = 1 + k
            in_specs.append(pl.BlockSpec(memory_space=pl.ANY))
            args.append(a)

    t_out = T - L if drop_first else T
    out_shape = [jax.ShapeDtypeStruct((S, t_out, D_MODEL), F32)] + \
        [jax.ShapeDtypeStruct(sh, F32) for sh in shapes]
    out_specs = [pl.BlockSpec((1, L, D_MODEL), (lambda s, c: (s, shift(c), 0)) if drop_first else blk)] + \
        [state_spec(sh) for sh in shapes]
    scratch = [
        pltpu.VMEM((L + QK_HIST, QK_CHUNK), F32),
        pltpu.VMEM((QK_HIST, 2 * D_ML), F32),
        pltpu.VMEM((L // 2, 2 * D_ML), U32),
    ]
    if fresh:
        scratch.append(pltpu.VMEM((ML_HEADS, ML_HEAD_DIM // 2, ML_HEAD_DIM), U32))
    else:
        scratch += _conv_scratch(L)
    outs = pl.pallas_call(
        functools.partial(_mlstm_kernel, L=L, pad=pad, fresh=fresh, first_block=first_block,
                          chained=chained),
        grid=(S, T // L),
        in_specs=in_specs,
        out_specs=out_specs,
        out_shape=out_shape,
        scratch_shapes=scratch,
        input_output_aliases=aliases,
        compiler_params=_params(2),
        name="mlstm_fresh" if fresh else "mlstm_carry",
    )(*args)
    return outs[0], outs[1:]


L_PROMPT = 256
TM_SAMPLE = 1024


def _layer_weights(l, norm_pre, norm_post, w_in_t, b_in, w_dw, b_dw, ln_g, ln_b, w_qk_conv, b_qk_conv,
                   f_bias, ml_norm, w_conv_out, w_ml_out, w_out):
    b = b_in[l]
    tail = w_in_t[l, LO_IF:, :]
    w_gate = jnp.pad(tail[:N_IF], ((0, GATE_W - N_IF), (0, 0)))[None]
    w_gc = tail[N_IF:N_IF + D_MODEL][None]
    w_gm = tail[N_IF + D_MODEL:][None]
    row = lambda a: a[None, :]
    return dict(
        norm_pre=row(norm_pre[l]), norm_post=row(norm_post[l]),
        w_pa=_pack_t(w_in_t, l, 0, D_PA), b_pa=row(b[:D_PA]),
        w_pb=_pack_t(w_in_t, l, D_PA, D_PB), b_pb=row(b[D_PA:LO_IF]),
        w_gate=_pack_t(w_gate, 0), b_gate=row(jnp.pad(b[LO_IF:LO_GC], (0, GATE_W - N_IF))),
        w_gc=_pack_t(w_gc, 0), b_gc=row(b[LO_GC:LO_GM]),
        w_gm=_pack_t(w_gm, 0), b_gm=row(b[LO_GM:]),
        w_dw=w_dw[l], b_dw=row(b_dw[l]), ln_g=row(ln_g[l]), ln_b=row(ln_b[l]),
        w_qkc=w_qk_conv[l], b_qkc=row(b_qk_conv[l]),
        f_bias=row(jnp.pad(f_bias[l], (ML_HEADS, GATE_W - N_IF))), ml_norm=row(ml_norm[l]),
        w_conv_out=_pack(w_conv_out, l), w_ml_out=_pack(w_ml_out, l), w_out=_pack(w_out, l))


def kernel(x_prompt, x_sample, state_conv, state_qk_conv, state_C, state_n, state_m, meta_tokens,
           norm_pre, norm_post, w_in, b_in, w_dw, b_dw, ln_g, ln_b, w_qk_conv, b_qk_conv, f_bias,
           ml_norm, w_conv_out, w_ml_out, w_out):
    bp, seq, _ = x_prompt.shape
    bs, dec_seq, _ = x_sample.shape
    depth = w_in.shape[0]
    L = L_PROMPT
    pad = L - N_META
    assert seq % L == 0 and pad % SUBLANES == 0 and dec_seq % (2 * SUBLANES) == 0

    meta = meta_tokens.astype(x_prompt.dtype)
    m_pad = jnp.pad(state_m, ((0, 0), (0, 0), (0, GATE_W - ML_HEADS)))[:, :, None, :]
    w_in_t = jnp.swapaxes(w_in, 1, 2)
    mstate_in = (state_qk_conv, state_C, state_n, m_pad)

    xp, xs = x_prompt, x_sample
    cb_p = st_p = st_s = None
    for l in range(depth):
        lw = _layer_weights(l, norm_pre, norm_post, w_in_t, b_in, w_dw, b_dw, ln_g, ln_b, w_qk_conv,
                            b_qk_conv, f_bias, ml_norm, w_conv_out, w_ml_out, w_out)
        first = meta if l == 0 else None
        gcb, cb_p = _conv_branch(xp, lw, l, depth, L, meta=first, prev_out=cb_p, pad=pad)
        xp, st_p = _mlstm_branch(xp, gcb, lw, l, depth, L, meta=first, prev_out=st_p, pad=pad,
                                 drop_first=(l == depth - 1))

        xs2 = xs.reshape(bs * dec_seq, D_MODEL)
        tok = lambda a: a.reshape(bs, dec_seq, a.shape[-1])
        rows = lambda a, c: jnp.concatenate([lw[a], lw[c]], axis=0)
        cols = lambda a, c: jnp.concatenate([lw[a], lw[c]], axis=1)
        (pa,) = _inproj(xs2, lw["norm_pre"], rows("w_pa", "w_gc"), cols("b_pa", "b_gc"), TM_SAMPLE)
        pb, pg = _inproj(xs2, lw["norm_pre"], rows("w_pb", "w_gm"), cols("b_pb", "b_gm"), TM_SAMPLE,
                         lw["w_gate"], lw["b_gate"])
        xs, st_s = _mlstm_branch(xs, None, lw, l, depth, dec_seq,
                                 proj=(tok(pb), tok(pg), tok(pa)),
                                 state=mstate_in + (state_conv,), prev_out=st_s)

    fin = lambda cb, st: (cb,) + tuple(st[:3]) + (st[3][:, :, 0, :ML_HEADS],)
    return (xp, xs) + fin(cb_p, st_p) + fin(st_s[4], st_s)
```

```python
import functools

import jax
import jax.numpy as jnp
from jax import lax
from jax.experimental import pallas as pl
from jax.experimental.pallas import tpu as pltpu

F32 = jnp.float32
BF16 = jnp.bfloat16
U32 = jnp.uint32

D_MODEL = 1024
D_CONV = D_MODEL
CONV_K = 31
D_ML = 2 * D_MODEL
ML_HEADS = 4
ML_HEAD_DIM = D_ML // ML_HEADS
QK_CONV_K = 4
N_META = 16
EPS = 1e-6

LANES = 128
SUBLANES = 8
GATE_W = LANES
N_IF = 2 * ML_HEADS
CONV_HIST = 32
CONV_BASE = CONV_HIST - (CONV_K - 1)
CONV_SPAN = (CONV_BASE + CONV_K - 2) // SUBLANES * SUBLANES
QK_HIST = SUBLANES
QK_BASE = QK_HIST - (QK_CONV_K - 1)
QK_CHUNK = D_ML
CONV_ROWS = 128
GLU_CHUNK = 2 * LANES
TN = 1024
VMEM_LIMIT = 62 * 1024 * 1024

D_PA = 3 * D_CONV
D_PB = 5 * D_ML
LO_IF = D_PA + D_PB
LO_GC = LO_IF + N_IF
LO_GM = LO_GC + D_MODEL
CA_A, CA_GL, CA_ZC = 0, D_CONV, 2 * D_CONV
CB_Q, CB_K, CB_V, CB_O, CB_ZM = 0, D_ML, 2 * D_ML, 3 * D_ML, 4 * D_ML


def _sigmoid(x):
    return jax.nn.sigmoid(x)


def _silu(x):
    return x * jax.nn.sigmoid(x)


def _rms(x, g):
    return x * lax.rsqrt(jnp.mean(x * x, axis=-1, keepdims=True) + EPS) * g


def _words(x_bf16):
    return pltpu.bitcast(x_bf16, U32)


def _halves(x_u32):
    return pltpu.bitcast(x_u32, BF16)


def _wdot(x_bf16, w_words):
    return jnp.dot(x_bf16, _halves(w_words), preferred_element_type=F32)


def _wdot_t(x_bf16, wt_words):
    return lax.dot_general(x_bf16, _halves(wt_words), (((1,), (1,)), ((), ())),
                           preferred_element_type=F32)


def _params(n_grid):
    return pltpu.CompilerParams(dimension_semantics=("arbitrary",) * n_grid,
                                vmem_limit_bytes=VMEM_LIMIT)


def _pack_kernel(w_ref, o_ref):
    o_ref[...] = _words(w_ref[...].astype(BF16))


def _pack(w3d, layer, col_lo=0, n_col=None):
    depth, k, n = w3d.shape
    n_col = n - col_lo if n_col is None else n_col
    tn = min(n_col, TN)
    assert col_lo % tn == 0 and n_col % tn == 0
    j0 = col_lo // tn
    return pl.pallas_call(
        _pack_kernel,
        grid=(n_col // tn,),
        in_specs=[pl.BlockSpec((k, tn), lambda j: (layer, j + j0))],
        out_specs=pl.BlockSpec((k // 2, tn), lambda j: (0, j)),
        out_shape=jax.ShapeDtypeStruct((k // 2, n_col), U32),
        compiler_params=_params(1),
        name="pack_weight",
    )(w3d.reshape(depth * k, n))


def _pack_t(wt3d, layer, row_lo=0, n_row=None):
    _, n, k = wt3d.shape
    n_row = n - row_lo if n_row is None else n_row
    tn = min(n_row, TN)
    assert row_lo % tn == 0 and n_row % tn == 0
    j0 = row_lo // tn
    return pl.pallas_call(
        _pack_kernel,
        grid=(n_row // tn,),
        in_specs=[pl.BlockSpec((None, tn, k), lambda j: (layer, j + j0, 0))],
        out_specs=pl.BlockSpec((tn // 2, k), lambda j: (j, 0)),
        out_shape=jax.ShapeDtypeStruct((n_row // 2, k), U32),
        compiler_params=_params(1),
        name="pack_weight_t",
    )(wt3d)


def _inproj_kernel(*refs, gated):
    if gated:
        x_ref, g_ref, w_ref, b_ref, wg_ref, bg_ref, o_ref, og_ref, h_ref = refs
    else:
        x_ref, g_ref, w_ref, b_ref, o_ref, h_ref = refs

    @pl.when(pl.program_id(1) == 0)
    def _():
        hb = _rms(x_ref[...], g_ref[...]).astype(BF16)
        h_ref[...] = _words(hb)
        if gated:
            og_ref[...] = _wdot_t(hb, wg_ref[...]) + bg_ref[...]

    o_ref[...] = _wdot_t(_halves(h_ref[...]), w_ref[...]) + b_ref[...]


def _inproj(x2d, g, wt_words, b, tm, w_gate=None, b_gate=None):
    n_tok, n_col = x2d.shape[0], 2 * wt_words.shape[0]
    gated = w_gate is not None
    in_specs = [
        pl.BlockSpec((tm, D_MODEL), lambda i, j: (i, 0)),
        pl.BlockSpec((1, D_MODEL), lambda i, j: (0, 0)),
        pl.BlockSpec((TN // 2, D_MODEL), lambda i, j: (j, 0)),
        pl.BlockSpec((1, TN), lambda i, j: (0, j)),
    ]
    out_specs = [pl.BlockSpec((tm, TN), lambda i, j: (i, j))]
    out_shape = [jax.ShapeDtypeStruct((n_tok, n_col), F32)]
    args = [x2d, g, wt_words, b]
    if gated:
        in_specs += [pl.BlockSpec((GATE_W // 2, D_MODEL), lambda i, j: (0, 0)),
                     pl.BlockSpec((1, GATE_W), lambda i, j: (0, 0))]
        out_specs.append(pl.BlockSpec((tm, GATE_W), lambda i, j: (i, 0)))
        out_shape.append(jax.ShapeDtypeStruct((n_tok, GATE_W), F32))
        args += [w_gate, b_gate]
    return pl.pallas_call(
        functools.partial(_inproj_kernel, gated=gated),
        grid=(n_tok // tm, n_col // TN),
        in_specs=in_specs, out_specs=out_specs, out_shape=out_shape,
        scratch_shapes=[pltpu.VMEM((tm // 2, D_MODEL), U32)],
        compiler_params=_params(2),
        name="inproj",
    )(*args)


def _cumsum_rows(x, n_rows):
    row = lax.broadcasted_iota(jnp.int32, x.shape, 0)
    k = 1
    while k < n_rows:
        x = x + jnp.where(row >= k, pltpu.roll(x, k, axis=0), 0.0)
        k *= 2
    return x


def _conv_block(proj_a, proj_gc, valid, cb_out, ubuf, ush, cvo, wdw, bdw, lng, lnb, wco, L):
    rc = min(L, CONV_ROWS)
    n_chunk = D_CONV // GLU_CHUNK
    for ck in range(n_chunk):
        co = ck * GLU_CHUNK
        u = proj_a(CA_A + co, GLU_CHUNK) * _sigmoid(proj_a(CA_GL + co, GLU_CHUNK))
        if valid is not None:
            u = jnp.where(valid, u, 0.0)
        ubuf[CONV_HIST:CONV_HIST + L, co:co + GLU_CHUNK] = u
        if ck == n_chunk - 2:
            zc = proj_a(CA_ZC, D_CONV)
        if ck == n_chunk - 1:
            gc = proj_gc(0, D_MODEL)
        for it in range(co // LANES, (co + GLU_CHUNK) // LANES):
            ls = slice(it * LANES, (it + 1) * LANES)
            sb = (it % 2) * (SUBLANES - 1)
            for sh in range(1, SUBLANES):
                ush[sb + sh - 1, :, :] = ubuf[sh:sh + L + CONV_SPAN, ls]
            for r0 in range(0, L, rc):
                acc = jnp.zeros((rc, LANES), F32) + bdw[:, ls]
                for j in range(CONV_K):
                    a8, sh = divmod(CONV_BASE + j, SUBLANES)
                    lo = a8 * SUBLANES + r0
                    src = ubuf[lo:lo + rc, ls] if sh == 0 else ush[sb + sh - 1, lo:lo + rc, :]
                    acc = acc + src * wdw[j:j + 1, ls]
                cvo[r0:r0 + rc, ls] = acc
    cb_out[0] = ubuf[L + CONV_BASE:L + CONV_HIST, :]
    ubuf[0:CONV_HIST, :] = ubuf[L:L + CONV_HIST, :]

    cv = cvo[...]
    mu = jnp.mean(cv, axis=-1, keepdims=True)
    xc = cv - mu
    ln = xc * lax.rsqrt(jnp.mean(xc * xc, axis=-1, keepdims=True) + EPS) * lng[...] + lnb[...]
    ua = _silu(ln) * _silu(zc)
    br_c = _wdot(ua.astype(BF16), wco[...])
    return _sigmoid(gc) * br_c


def _conv_scratch(L):
    return [
        pltpu.VMEM((L + CONV_HIST, D_CONV), F32),
        pltpu.VMEM((2 * (SUBLANES - 1), L + CONV_SPAN, LANES), F32),
        pltpu.VMEM((L, D_CONV), F32),
    ]


def _conv_kernel(*refs, L, pad, fresh, first_block, chained):
    refs = list(refs)
    x_ref = refs.pop(0) if fresh else None
    meta_ref = refs.pop(0) if first_block else None
    if fresh:
        npre, wpa, bpa, wgc, bgc = (refs.pop(0) for _ in range(5))
    else:
        pa_ref, pgc_ref, cb_in = (refs.pop(0) for _ in range(3))
    wdw, bdw, lng, lnb, wco = (refs.pop(0) for _ in range(5))
    if chained:
        refs.pop(0)
    gcb_ref, cb_out = refs.pop(0), refs.pop(0)
    ubuf, ush, cvo = (refs.pop(0) for _ in range(3))

    c = pl.program_id(1)

    @pl.when(c == 0)
    def _init():
        ubuf[0:CONV_HIST, :] = jnp.zeros((CONV_HIST, D_CONV), F32)
        if not fresh:
            ubuf[CONV_BASE:CONV_HIST, :] = cb_in[0]

    if fresh:
        x = x_ref[0]
        if first_block:
            x0 = jnp.concatenate([jnp.zeros((pad, D_MODEL), F32), meta_ref[...]], axis=0)
            x = jnp.where(c == 0, x0, x)
        h = _rms(x, npre[...]).astype(BF16)

        def proj_a(off, width):
            return _wdot_t(h, wpa[off // 2:(off + width) // 2, :]) + bpa[:, off:off + width]

        def proj_gc(off, width):
            return _wdot_t(h, wgc[off // 2:(off + width) // 2, :]) + bgc[:, off:off + width]
    else:
        def proj_a(off, width):
            return pa_ref[0, :, off:off + width]

        def proj_gc(off, width):
            return pgc_ref[0, :, off:off + width]

    valid = None
    if pad:
        row_id = lax.broadcasted_iota(jnp.int32, (L, 1), 0) + c * L
        valid = row_id >= pad
    gcb_ref[0] = _conv_block(proj_a, proj_gc, valid, cb_out, ubuf, ush, cvo, wdw, bdw, lng, lnb, wco, L)


def _conv_branch(x, lw, layer, depth, L, *, meta=None, proj=None, state=None, prev_out=None, pad=0):
    fresh = proj is None
    first_block = meta is not None
    chained = prev_out is not None
    src = x if fresh else proj[0]
    S = src.shape[0]
    T = src.shape[1] + (L if first_block else 0)
    shift = lambda c: jnp.maximum(c - 1, 0)
    blk = lambda s, c: (s, c, 0)
    full2 = lambda s, c: (0, 0)
    const = dict(pipeline_mode=pl.Buffered(1))
    st_shape = (depth, S, CONV_K - 1, D_CONV)
    st_spec = pl.BlockSpec((None, 1) + st_shape[2:], lambda s, c: (layer, s, 0, 0))

    in_specs, args = [], []
    if fresh:
        in_specs.append(pl.BlockSpec((1, L, D_MODEL), (lambda s, c: (s, shift(c), 0)) if first_block
                                     else blk))
        args.append(x)
        if first_block:
            in_specs.append(pl.BlockSpec(meta.shape, full2, **const))
            args.append(meta)
        ws = [lw["norm_pre"], lw["w_pa"], lw["b_pa"], lw["w_gc"], lw["b_gc"]]
    else:
        in_specs += [pl.BlockSpec((1, L, D_PA), blk), pl.BlockSpec((1, L, D_MODEL), blk), st_spec]
        args += [proj[0], proj[1], state]
        ws = []
    ws += [lw["w_dw"], lw["b_dw"], lw["ln_g"], lw["ln_b"], lw["w_conv_out"]]
    in_specs += [pl.BlockSpec(w.shape, full2, **const) for w in ws]
    args += ws
    aliases = {}
    if chained:
        aliases[len(args)] = 1
        in_specs.append(pl.BlockSpec(memory_space=pl.ANY))
        args.append(prev_out)

    return pl.pallas_call(
        functools.partial(_conv_kernel, L=L, pad=pad, fresh=fresh, first_block=first_block,
                          chained=chained),
        grid=(S, T // L),
        in_specs=in_specs,
        out_specs=[pl.BlockSpec((1, L, D_MODEL), blk), st_spec],
        out_shape=[jax.ShapeDtypeStruct((S, T, D_MODEL), F32), jax.ShapeDtypeStruct(st_shape, F32)],
        scratch_shapes=_conv_scratch(L),
        input_output_aliases=aliases,
        compiler_params=_params(2),
        name="conv_fresh" if fresh else "conv_carry",
    )(*args)


N_MSTATE = 4


def _mlstm_kernel(*refs, L, pad, fresh, first_block, chained):
    refs = list(refs)
    x_ref = refs.pop(0)
    meta_ref = refs.pop(0) if first_block else None
    if fresh:
        gcb_ref = refs.pop(0)
        npre, wpb, bpb, wgm, bgm, wgate, bgate = (refs.pop(0) for _ in range(7))
    else:
        pb_ref, pgm_ref, pg_ref, pa_ref, pgc_ref = (refs.pop(0) for _ in range(5))
        qb_in, c_in, n_in, m_in, cb_in = (refs.pop(0) for _ in range(N_MSTATE + 1))
    wqk, bqk, fb, mln, wmo, wo, npost = (refs.pop(0) for _ in range(7))
    if not fresh:
        wdw, bdw, lng, lnb, wco = (refs.pop(0) for _ in range(5))
    n_state = N_MSTATE if fresh else N_MSTATE + 1
    if chained:
        del refs[:n_state]
    y_ref, qb_out, c_out, n_out, m_out = (refs.pop(0) for _ in range(1 + N_MSTATE))
    cb_out = None if fresh else refs.pop(0)
    qkc, qkh, qks = (refs.pop(0) for _ in range(3))
    if fresh:
        cw = refs.pop(0)
    else:
        ubuf, ush, cvo = (refs.pop(0) for _ in range(3))

    c = pl.program_id(1)

    @pl.when(c == 0)
    def _init():
        qkh[...] = jnp.zeros(qkh.shape, F32)
        if fresh:
            c_out[...] = jnp.zeros(c_out.shape, F32)
            n_out[...] = jnp.zeros(n_out.shape, F32)
            m_out[...] = jnp.zeros(m_out.shape, F32)
            cw[...] = jnp.zeros(cw.shape, U32)
        else:
            qkh[QK_BASE:QK_HIST, :] = qb_in[0]
            ubuf[0:CONV_HIST, :] = jnp.zeros((CONV_HIST, D_CONV), F32)
            ubuf[CONV_BASE:CONV_HIST, :] = cb_in[0]

    if fresh:
        m_prev = m_out[0]
        c_prev = lambda h_i: c_out[0, h_i]
        c_prev_bf16 = lambda h_i: _halves(cw[h_i])
        n_prev = lambda h_i: n_out[0, h_i:h_i + 1, :]
    else:
        m_prev = m_in[0]
        c_prev = lambda h_i: c_in[0, h_i]
        c_prev_bf16 = lambda h_i: c_in[0, h_i].astype(BF16)
        n_prev = lambda h_i: n_in[0, h_i:h_i + 1, :]

    def x_block():
        x = x_ref[0]
        if first_block:
            x0 = jnp.concatenate([jnp.zeros((pad, D_MODEL), F32), meta_ref[...]], axis=0)
            x = jnp.where(c == 0, x0, x)
        return x

    if fresh:
        h = _rms(x_block(), npre[...]).astype(BF16)

        def proj(off, width):
            return _wdot_t(h, wpb[off // 2:(off + width) // 2, :]) + bpb[:, off:off + width]

        g = _wdot_t(h, wgate[...]) + bgate[...]
    else:
        def proj(off, width):
            return pb_ref[0, :, off:off + width]

        g = pg_ref[0]

    row_id = lax.broadcasted_iota(jnp.int32, (L, 1), 0) + c * L
    valid = row_id >= pad

    def qk_chunk(it):
        cs = slice(it * QK_CHUNK, (it + 1) * QK_CHUNK)
        qk_pre = proj(CB_Q + it * QK_CHUNK, QK_CHUNK)
        if pad:
            qk_pre = jnp.where(valid, qk_pre, 0.0)
        qkc[0:QK_HIST, :] = qkh[:, cs]
        qkc[QK_HIST:QK_HIST + L, :] = qk_pre
        acc = qk_pre * wqk[QK_CONV_K - 1:QK_CONV_K, cs] + bqk[:, cs]
        for j in range(QK_CONV_K - 1):
            acc = acc + qkc[QK_BASE + j:QK_BASE + j + L, :] * wqk[j:j + 1, cs]
        qkh[:, cs] = qkc[L:L + QK_HIST, :]
        acc = _silu(acc)
        if it * QK_CHUNK >= D_ML:
            acc = acc * (ML_HEAD_DIM ** -0.5)
        qks[:, cs] = _words(acc.astype(BF16))

    for it in range(2 * D_ML // QK_CHUNK):
        qk_chunk(it)
    qb_out[0] = qkh[QK_BASE:QK_HIST, :]

    logf = jax.nn.log_sigmoid(g + fb[...])
    if pad:
        logf = jnp.where(valid, logf, 0.0)
    b_al = pltpu.roll(_cumsum_rows(logf, L), GATE_W - ML_HEADS, axis=1)
    inter = b_al + m_prev
    d_src = g - b_al
    d_src_t = d_src.T
    tt = lax.broadcasted_iota(jnp.int32, (L, L), 0)
    ss = lax.broadcasted_iota(jnp.int32, (L, L), 1)
    mask = ss <= tt
    if pad:
        mask = mask & ((ss + c * L) >= pad)

    hms, updates = [], []
    for h_i in range(ML_HEADS):
        hs = slice(h_i * ML_HEAD_DIM, (h_i + 1) * ML_HEAD_DIM)
        v = proj(CB_V + h_i * ML_HEAD_DIM, ML_HEAD_DIM)
        og = proj(CB_O + h_i * ML_HEAD_DIM, ML_HEAD_DIM)
        zm = proj(CB_ZM + h_i * ML_HEAD_DIM, ML_HEAD_DIM)
        q = _halves(qks[:, hs])
        k = _halves(qks[:, D_ML + h_i * ML_HEAD_DIM:D_ML + (h_i + 1) * ML_HEAD_DIM])
        dmat = jnp.where(mask, b_al[:, h_i:h_i + 1] + d_src_t[h_i:h_i + 1, :], -jnp.inf)
        inter_h = inter[:, h_i:h_i + 1]
        m_t = jnp.maximum(inter_h, jnp.max(dmat, axis=-1, keepdims=True))
        w_intra = jnp.exp(dmat - m_t)
        w_inter = jnp.exp(inter_h - m_t)
        s_mat = lax.dot_general(q, k, (((1,), (1,)), ((), ())), preferred_element_type=F32) * w_intra
        n_old = n_prev(h_i)
        qc = lax.dot_general(q, c_prev_bf16(h_i), (((1,), (1,)), ((), ())), preferred_element_type=F32)
        num = jnp.dot(s_mat.astype(BF16), v.astype(BF16), preferred_element_type=F32) + w_inter * qc
        qn = jnp.sum(q.astype(F32) * n_old, axis=-1, keepdims=True)
        den = jnp.sum(s_mat, axis=-1, keepdims=True) + w_inter * qn
        hh = num * (1.0 / jnp.maximum(jnp.abs(den), jnp.exp(-m_t)))
        hh = hh * lax.rsqrt(jnp.mean(hh * hh, axis=-1, keepdims=True) + EPS) * mln[:, hs]
        hm = hh * _sigmoid(og) * _silu(zm)
        hms.append(hm.astype(BF16))

        m_new = m_t[L - 1:L, :]
        decay = jnp.exp(b_al[L - 1:L, h_i:h_i + 1] + m_prev[:, h_i:h_i + 1] - m_new)
        w_s = jnp.exp(d_src[:, h_i:h_i + 1] + (b_al[L - 1:L, h_i:h_i + 1] - m_new))
        if pad:
            w_s = jnp.where(valid, w_s, 0.0)
        vw = (w_s * v).astype(BF16)
        n_new = decay * n_old + jnp.sum(w_s * k.astype(F32), axis=0, keepdims=True)
        updates.append((m_new, decay, vw, k, n_new))

    if fresh:
        gm = _wdot_t(h, wgm[...]) + bgm[...]
    else:
        gm = pgm_ref[0]
    br_m = _wdot(jnp.concatenate(hms, axis=1), wmo[...])
    if fresh:
        gcb = gcb_ref[0]
    else:
        gcb = _conv_block(lambda off, width: pa_ref[0, :, off:off + width],
                          lambda off, width: pgc_ref[0, :, off:off + width],
                          None, cb_out, ubuf, ush, cvo, wdw, bdw, lng, lnb, wco, L)
    ym = gcb + _sigmoid(gm) * br_m
    out = _wdot(ym.astype(BF16), wo[...])
    y_ref[0] = x_block() + _rms(out, npost[...])

    lane = lax.broadcasted_iota(jnp.int32, (1, GATE_W), 1)
    m_row = jnp.zeros((1, GATE_W), F32)
    for h_i, (m_new, decay, vw, k, n_new) in enumerate(updates):
        c_new = decay * c_prev(h_i) + lax.dot_general(
            vw, k, (((0,), (0,)), ((), ())), preferred_element_type=F32)
        c_out[0, h_i] = c_new
        if fresh:
            cw[h_i] = _words(c_new.astype(BF16))
        n_out[0, h_i:h_i + 1, :] = n_new
        m_row = jnp.where(lane == h_i, m_new, m_row)
    m_out[0] = m_row


def _mstate_shapes(depth, S):
    return [
        (depth, S, QK_CONV_K - 1, 2 * D_ML),
        (depth, S, ML_HEADS, ML_HEAD_DIM, ML_HEAD_DIM),
        (depth, S, ML_HEADS, ML_HEAD_DIM),
        (depth, S, 1, GATE_W),
    ]


def _mlstm_branch(x, gcb, lw, layer, depth, L, *, meta=None, proj=None, state=None, prev_out=None,
                  pad=0, drop_first=False):
    fresh = proj is None
    first_block = meta is not None
    chained = prev_out is not None
    S = x.shape[0]
    T = gcb.shape[1] if fresh else x.shape[1]
    assert fresh or T == L
    shapes = _mstate_shapes(depth, S)
    if not fresh:
        shapes.append((depth, S, CONV_K - 1, D_CONV))
    shift = lambda c: jnp.maximum(c - 1, 0)
    blk = lambda s, c: (s, c, 0)
    full2 = lambda s, c: (0, 0)
    const = dict(pipeline_mode=pl.Buffered(1))

    def state_spec(shape, **kw):
        nd = len(shape) - 2
        return pl.BlockSpec((None, 1) + shape[2:], lambda s, c: (layer, s) + (0,) * nd, **kw)

    in_specs = [pl.BlockSpec((1, L, D_MODEL), (lambda s, c: (s, shift(c), 0)) if first_block else blk)]
    args = [x]
    if first_block:
        in_specs.append(pl.BlockSpec(meta.shape, full2, **const))
        args.append(meta)
    if fresh:
        in_specs.append(pl.BlockSpec((1, L, D_MODEL), blk))
        args.append(gcb)
        ws = [lw["norm_pre"], lw["w_pb"], lw["b_pb"], lw["w_gm"], lw["b_gm"], lw["w_gate"], lw["b_gate"]]
    else:
        in_specs += [pl.BlockSpec((1, L, p.shape[-1]), blk) for p in proj]
        args += list(proj)
        in_specs += [state_spec(sh) for sh in shapes]
        args += list(state)
        ws = []
    ws += [lw["w_qkc"], lw["b_qkc"], lw["f_bias"], lw["ml_norm"], lw["w_ml_out"], lw["w_out"],
           lw["norm_post"]]
    if not fresh:
        ws += [lw["w_dw"], lw["b_dw"], lw["ln_g"], lw["ln_b"], lw["w_conv_out"]]
    in_specs += [pl.BlockSpec(w.shape, full2, **const) for w in ws]
    args += ws
    aliases = {}
    if chained:
        for k, a in enumerate(prev_out):
            aliases[len(args)] = 1 + k
            in_specs.append(pl.BlockSpec(memory_space=pl.ANY))
            args.append(a)

    t_out = T - L if drop_first else T
    out_shape = [jax.ShapeDtypeStruct((S, t_out, D_MODEL), F32)] + \
        [jax.ShapeDtypeStruct(sh, F32) for sh in shapes]
    out_specs = [pl.BlockSpec((1, L, D_MODEL), (lambda s, c: (s, shift(c), 0)) if drop_first else blk)] + \
        [state_spec(sh) for sh in shapes]
    scratch = [
        pltpu.VMEM((L + QK_HIST, QK_CHUNK), F32),
        pltpu.VMEM((QK_HIST, 2 * D_ML), F32),
        pltpu.VMEM((L // 2, 2 * D_ML), U32),
    ]
    if fresh:
        scratch.append(pltpu.VMEM((ML_HEADS, ML_HEAD_DIM // 2, ML_HEAD_DIM), U32))
    else:
        scratch += _conv_scratch(L)
    outs = pl.pallas_call(
        functools.partial(_mlstm_kernel, L=L, pad=pad, fresh=fresh, first_block=first_block,
                          chained=chained),
        grid=(S, T // L),
        in_specs=in_specs,
        out_specs=out_specs,
        out_shape=out_shape,
        scratch_shapes=scratch,
        input_output_aliases=aliases,
        compiler_params=_params(2),
        name="mlstm_fresh" if fresh else "mlstm_carry",
    )(*args)
    return outs[0], outs[1:]


L_PROMPT = 256
TM_SAMPLE = 2048


def _layer_weights(l, norm_pre, norm_post, w_in_t, b_in, w_dw, b_dw, ln_g, ln_b, w_qk_conv, b_qk_conv,
                   f_bias, ml_norm, w_conv_out, w_ml_out, w_out):
    b = b_in[l]
    tail = w_in_t[l, LO_IF:, :]
    w_gate = jnp.pad(tail[:N_IF], ((0, GATE_W - N_IF), (0, 0)))[None]
    w_gc = tail[N_IF:N_IF + D_MODEL][None]
    w_gm = tail[N_IF + D_MODEL:][None]
    row = lambda a: a[None, :]
    return dict(
        norm_pre=row(norm_pre[l]), norm_post=row(norm_post[l]),
        w_pa=_pack_t(w_in_t, l, 0, D_PA), b_pa=row(b[:D_PA]),
        w_pb=_pack_t(w_in_t, l, D_PA, D_PB), b_pb=row(b[D_PA:LO_IF]),
        w_gate=_pack_t(w_gate, 0), b_gate=row(jnp.pad(b[LO_IF:LO_GC], (0, GATE_W - N_IF))),
        w_gc=_pack_t(w_gc, 0), b_gc=row(b[LO_GC:LO_GM]),
        w_gm=_pack_t(w_gm, 0), b_gm=row(b[LO_GM:]),
        w_dw=w_dw[l], b_dw=row(b_dw[l]), ln_g=row(ln_g[l]), ln_b=row(ln_b[l]),
        w_qkc=w_qk_conv[l], b_qkc=row(b_qk_conv[l]),
        f_bias=row(jnp.pad(f_bias[l], (ML_HEADS, GATE_W - N_IF))), ml_norm=row(ml_norm[l]),
        w_conv_out=_pack(w_conv_out, l), w_ml_out=_pack(w_ml_out, l), w_out=_pack(w_out, l))


def kernel(x_prompt, x_sample, state_conv, state_qk_conv, state_C, state_n, state_m, meta_tokens,
           norm_pre, norm_post, w_in, b_in, w_dw, b_dw, ln_g, ln_b, w_qk_conv, b_qk_conv, f_bias,
           ml_norm, w_conv_out, w_ml_out, w_out):
    bp, seq, _ = x_prompt.shape
    bs, dec_seq, _ = x_sample.shape
    depth = w_in.shape[0]
    L = L_PROMPT
    pad = L - N_META
    assert seq % L == 0 and pad % SUBLANES == 0 and dec_seq % (2 * SUBLANES) == 0

    meta = meta_tokens.astype(x_prompt.dtype)
    m_pad = jnp.pad(state_m, ((0, 0), (0, 0), (0, GATE_W - ML_HEADS)))[:, :, None, :]
    w_in_t = jnp.swapaxes(w_in, 1, 2)
    mstate_in = (state_qk_conv, state_C, state_n, m_pad)

    xp, xs = x_prompt, x_sample
    cb_p = st_p = st_s = None
    for l in range(depth):
        lw = _layer_weights(l, norm_pre, norm_post, w_in_t, b_in, w_dw, b_dw, ln_g, ln_b, w_qk_conv,
                            b_qk_conv, f_bias, ml_norm, w_conv_out, w_ml_out, w_out)
        first = meta if l == 0 else None
        gcb, cb_p = _conv_branch(xp, lw, l, depth, L, meta=first, prev_out=cb_p, pad=pad)
        xp, st_p = _mlstm_branch(xp, gcb, lw, l, depth, L, meta=first, prev_out=st_p, pad=pad,
                                 drop_first=(l == depth - 1))

        xs2 = xs.reshape(bs * dec_seq, D_MODEL)
        tok = lambda a: a.reshape(bs, dec_seq, a.shape[-1])
        (pa,) = _inproj(xs2, lw["norm_pre"], lw["w_pa"], lw["b_pa"], TM_SAMPLE)
        (pgc,) = _inproj(xs2, lw["norm_pre"], lw["w_gc"], lw["b_gc"], TM_SAMPLE)
        pb, pg = _inproj(xs2, lw["norm_pre"], lw["w_pb"], lw["b_pb"], TM_SAMPLE, lw["w_gate"], lw["b_gate"])
        (pgm,) = _inproj(xs2, lw["norm_pre"], lw["w_gm"], lw["b_gm"], TM_SAMPLE)
        xs, st_s = _mlstm_branch(xs, None, lw, l, depth, dec_seq,
                                 proj=(tok(pb), tok(pgm), tok(pg), tok(pa), tok(pgc)),
                                 state=mstate_in + (state_conv,), prev_out=st_s)

    fin = lambda cb, st: (cb,) + tuple(st[:3]) + (st[3][:, :, 0, :ML_HEADS],)
    return (xp, xs) + fin(cb_p, st_p) + fin(st_s[4], st_s)
```
